```python
import math
import jax, jax.numpy as jnp
from jax import lax
import numpy as np

D_MODEL = 1024
BATCH = 32
SEQ = 2048
DEPTH = 2

A_HEADS = 4
A_HEAD_DIM = 64
B_HEADS = 4
B_HEAD_DIM = 128
B_PATTERNS = ((128, 1), (512, 4), (2048, 16))
C_HEADS = 8
C_NOPE_DIM = 64
C_ROPE_DIM = 32
C_V_DIM = 64
C_Q_LORA = 384
C_KV_LORA = 256
ROPE_THETA = 10000.0
FFN_HIDDEN = -(-8 * D_MODEL // (3 * 256)) * 256
A_QK_W = A_HEADS * 2 * A_HEAD_DIM
A_V_W = A_HEADS * 2 * A_HEAD_DIM
B_W = B_HEADS * B_HEAD_DIM
C_DKV_W = C_KV_LORA + C_ROPE_DIM
C_OUT_W = C_HEADS * C_V_DIM
N_BRANCHES = 3
IN_SPLITS = (A_QK_W, A_QK_W, A_V_W, B_W, B_W, B_W, C_Q_LORA, C_DKV_W, N_BRANCHES * D_MODEL)
IN_WIDTH = A_QK_W * 2 + A_V_W + B_W * 3 + C_Q_LORA + C_DKV_W + N_BRANCHES * D_MODEL
QBLOCK = 128
NORM_EPS = 1e-6
NEG_INF = -1e30

kernel_name = "hybrid_gated_diff_dilated_mla_block"


def _rms_norm(x, g):
    xf = x.astype(jnp.float32)
    y = xf * lax.rsqrt(jnp.mean(xf * xf, axis=-1, keepdims=True) + NORM_EPS)
    return (y * g.astype(jnp.float32)).astype(x.dtype)


def _alibi_slopes():
    n = A_HEADS + B_HEADS
    s = 2.0 ** (-8.0 * jnp.arange(1, n + 1, dtype=jnp.float32) / n)
    return s[0::2], s[1::2]


def _rope(x, pos):
    half = x.shape[-1] // 2
    inv_freq = ROPE_THETA ** (-jnp.arange(half, dtype=jnp.float32) / half)
    ang = pos.astype(jnp.float32)[:, None] * inv_freq[None, :]
    cos, sin = jnp.cos(ang)[:, None, :], jnp.sin(ang)[:, None, :]
    xf = x.astype(jnp.float32)
    x1, x2 = xf[..., :half], xf[..., half:]
    return jnp.concatenate([x1 * cos - x2 * sin, x1 * sin + x2 * cos], axis=-1).astype(x.dtype)


def _block_geometry(start, end):
    qpos = jnp.arange(start, end)
    kpos = jnp.arange(end)
    dist = (qpos[:, None] - kpos[None, :]).astype(jnp.float32)
    return dist >= 0, dist


def _causal_block_sweep(block_fn, seq):
    qb = min(QBLOCK, seq)
    return jnp.concatenate([block_fn(s, min(s + qb, seq)) for s in range(0, seq, qb)], axis=1)


def _diff_attention(q, k, v, lam, lam_init, slopes, gain):
    B, S = q.shape[:2]
    q = q.reshape(B, S, A_HEADS, 2, A_HEAD_DIM)
    k = k.reshape(B, S, A_HEADS, 2, A_HEAD_DIM)
    v = v.reshape(B, S, A_HEADS, 2 * A_HEAD_DIM)
    scale = A_HEAD_DIM ** -0.5

    def block(start, end):
        causal, dist = _block_geometry(start, end)
        s = jnp.einsum('bqhcd,bkhcd->bhcqk', q[:, start:end], k[:, :end]).astype(jnp.float32) * scale
        s = s - slopes[:, None, None, None] * dist
        s = jnp.where(causal, s, NEG_INF)
        p = jax.nn.softmax(s, axis=-1)
        p = p[:, :, 0] - lam * p[:, :, 1]
        return jnp.einsum('bhqk,bkhd->bqhd', p.astype(v.dtype), v[:, :end])

    o = _causal_block_sweep(block, S)
    o = _rms_norm(o, gain) * (1.0 - lam_init)
    return o.reshape(B, S, A_V_W)


def _strided_window_attention(q, k, v, window, dil, slopes):
    B, S, H, D = q.shape
    L = S // dil
    nw = window // dil

    def sub(t):
        return t.reshape(B, L, dil, H, D).transpose(0, 2, 1, 3, 4).reshape(B * dil, L, H, D)

    qs, ks, vs = sub(q), sub(k), sub(v)
    Bp = B * dil
    bq = min(QBLOCK, L)
    nb = L // bq
    n_prev = min(-(-nw // bq), nb - 1)
    kb_len = (n_prev + 1) * bq

    def band(t):
        tp = jnp.pad(t, ((0, 0), (n_prev * bq, 0), (0, 0), (0, 0)))
        views = [tp[:, j * bq: j * bq + L].reshape(Bp, nb, bq, H, D) for j in range(n_prev + 1)]
        return jnp.concatenate(views, axis=2)

    qb = qs.reshape(Bp, nb, bq, H, D)
    kb, vb = band(ks), band(vs)
    qi = jnp.arange(L).reshape(nb, bq)
    ki = (jnp.arange(nb)[:, None] - n_prev) * bq + jnp.arange(kb_len)[None, :]
    delta = qi[:, :, None] - ki[:, None, :]
    valid = (delta >= 0) & (delta <= nw) & (ki[:, None, :] >= 0)
    dist = (delta * dil).astype(jnp.float32)
    scale = D ** -0.5
    s = jnp.einsum('bnqhd,bnkhd->bhnqk', qb, kb).astype(jnp.float32) * scale
    s = s - slopes[:, None, None, None] * dist
    s = jnp.where(valid, s, NEG_INF)
    lse = jax.nn.logsumexp(s, axis=-1)
    p = jnp.exp(s - lse[..., None])
    o = jnp.einsum('bhnqk,bnkhd->bnqhd', p.astype(v.dtype), vb).reshape(B, dil, L, H, D)
    o = o.transpose(0, 2, 1, 3, 4).reshape(B, S, H, D)
    lse = lse.transpose(0, 2, 3, 1).reshape(B, dil, L, H).transpose(0, 2, 1, 3).reshape(B, S, H)
    return o, lse


def _dilated_attention(q, k, v, slopes):
    B, S = q.shape[:2]
    q = q.reshape(B, S, B_HEADS, B_HEAD_DIM)
    k = k.reshape(B, S, B_HEADS, B_HEAD_DIM)
    v = v.reshape(B, S, B_HEADS, B_HEAD_DIM)
    outs, lses = [], []
    for window, dil in B_PATTERNS:
        o, lse = _strided_window_attention(q, k, v, window, dil, slopes)
        outs.append(o)
        lses.append(lse)
    o = jnp.stack(outs, axis=0)
    w = jax.nn.softmax(jnp.stack(lses, axis=0), axis=0)
    o = jnp.sum(w[..., None].astype(o.dtype) * o, axis=0)
    return o.reshape(B, S, B_W)


def _mla(c_q, c_kv_pe, q_norm, w_uq, kv_norm, w_ukv, pos):
    B, S = c_q.shape[:2]
    q = (_rms_norm(c_q, q_norm) @ w_uq).reshape(B, S, C_HEADS, C_NOPE_DIM + C_ROPE_DIM)
    q_nope, q_pe = q[..., :C_NOPE_DIM], _rope(q[..., C_NOPE_DIM:], pos)
    c_kv, k_pe = c_kv_pe[..., :C_KV_LORA], c_kv_pe[..., C_KV_LORA:]
    k_pe = _rope(k_pe[:, :, None, :], pos)[:, :, 0]
    kv = (_rms_norm(c_kv, kv_norm) @ w_ukv).reshape(B, S, C_HEADS, C_NOPE_DIM + C_V_DIM)
    k_nope, v = kv[..., :C_NOPE_DIM], kv[..., C_NOPE_DIM:]
    scale = (C_NOPE_DIM + C_ROPE_DIM) ** -0.5

    def block(start, end):
        causal, _ = _block_geometry(start, end)
        s = (jnp.einsum('bqhd,bkhd->bhqk', q_nope[:, start:end], k_nope[:, :end])
             + jnp.einsum('bqhr,bkr->bhqk', q_pe[:, start:end], k_pe[:, :end])).astype(jnp.float32) * scale
        s = jnp.where(causal, s, NEG_INF)
        p = jax.nn.softmax(s, axis=-1)
        return jnp.einsum('bhqk,bkhd->bqhd', p.astype(v.dtype), v[:, :end])

    o = _causal_block_sweep(block, S)
    return o.reshape(B, S, C_OUT_W)


def _hybrid_layer(x, pos, slopes_a, slopes_b, layer_idx, attn_norm, w_in, diff_lambda, diff_norm,
                  mla_q_norm, mla_w_uq, mla_kv_norm, mla_w_ukv, w_branch_a, w_branch_b, w_branch_c,
                  w_out, ffn_norm, w_ffn_gate, w_ffn_up, w_ffn_down):
    h = _rms_norm(x, attn_norm)
    proj = h @ w_in
    offsets = np.cumsum(IN_SPLITS)[:-1].tolist()
    aq, ak, av, bq, bk, bv, c_q, c_kv_pe, gates = jnp.split(proj, offsets, axis=-1)

    lam_init = 0.8 - 0.6 * math.exp(-0.3 * layer_idx)
    lf = diff_lambda.astype(jnp.float32)
    lam = jnp.exp(jnp.sum(lf[0] * lf[1])) - jnp.exp(jnp.sum(lf[2] * lf[3])) + lam_init

    y_a = _diff_attention(aq, ak, av, lam, lam_init, slopes_a, diff_norm) @ w_branch_a
    y_b = _dilated_attention(bq, bk, bv, slopes_b) @ w_branch_b
    y_c = _mla(c_q, c_kv_pe, mla_q_norm, mla_w_uq, mla_kv_norm, mla_w_ukv, pos) @ w_branch_c
    g_a, g_b, g_c = jnp.split(jax.nn.sigmoid(gates), N_BRANCHES, axis=-1)
    x = x + (g_a * y_a + g_b * y_b + g_c * y_c) @ w_out

    h2 = _rms_norm(x, ffn_norm)
    x = x + (jax.nn.silu(h2 @ w_ffn_gate) * (h2 @ w_ffn_up)) @ w_ffn_down
    return x


def setup_inputs(seed: int = 0) -> dict:
    key = jax.random.key(seed)
    ks = jax.random.split(key, 20)

    def nrm(k, shape, scale):
        return jax.random.normal(k, shape, jnp.float32) * scale

    def gain(k, shape):
        return 1.0 + 0.02 * jax.random.normal(k, shape, jnp.float32)

    return {
        "x": nrm(ks[0], (BATCH, SEQ, D_MODEL), 1.0),
        "attn_norm": gain(ks[1], (DEPTH, D_MODEL)),
        "w_in": nrm(ks[2], (DEPTH, D_MODEL, IN_WIDTH), D_MODEL ** -0.5),
        "diff_lambda": nrm(ks[3], (DEPTH, 4, A_HEAD_DIM), 0.1),
        "diff_norm": gain(ks[4], (DEPTH, 2 * A_HEAD_DIM)),
        "mla_q_norm": gain(ks[5], (DEPTH, C_Q_LORA)),
        "mla_w_uq": nrm(ks[6], (DEPTH, C_Q_LORA, C_HEADS * (C_NOPE_DIM + C_ROPE_DIM)), C_Q_LORA ** -0.5),
        "mla_kv_norm": gain(ks[7], (DEPTH, C_KV_LORA)),
        "mla_w_ukv": nrm(ks[8], (DEPTH, C_KV_LORA, C_HEADS * (C_NOPE_DIM + C_V_DIM)), C_KV_LORA ** -0.5),
        "w_branch_a": nrm(ks[9], (DEPTH, A_V_W, D_MODEL), A_V_W ** -0.5),
        "w_branch_b": nrm(ks[10], (DEPTH, B_W, D_MODEL), B_W ** -0.5),
        "w_branch_c": nrm(ks[11], (DEPTH, C_OUT_W, D_MODEL), C_OUT_W ** -0.5),
        "w_out": nrm(ks[12], (DEPTH, D_MODEL, D_MODEL), D_MODEL ** -0.5),
        "ffn_norm": gain(ks[13], (DEPTH, D_MODEL)),
        "w_ffn_gate": nrm(ks[14], (DEPTH, D_MODEL, FFN_HIDDEN), D_MODEL ** -0.5),
        "w_ffn_up": nrm(ks[15], (DEPTH, D_MODEL, FFN_HIDDEN), D_MODEL ** -0.5),
        "w_ffn_down": nrm(ks[16], (DEPTH, FFN_HIDDEN, D_MODEL), FFN_HIDDEN ** -0.5),
        "final_norm": gain(ks[17], (D_MODEL,)),
    }


def reference(x, attn_norm, w_in, diff_lambda, diff_norm, mla_q_norm, mla_w_uq, mla_kv_norm,
              mla_w_ukv, w_branch_a, w_branch_b, w_branch_c, w_out, ffn_norm, w_ffn_gate,
              w_ffn_up, w_ffn_down, final_norm):
    pos = jnp.arange(x.shape[1], dtype=jnp.int32)
    slopes_a, slopes_b = _alibi_slopes()
    for l in range(DEPTH):
        x = _hybrid_layer(x, pos, slopes_a, slopes_b, l, attn_norm[l], w_in[l], diff_lambda[l],
                          diff_norm[l], mla_q_norm[l], mla_w_uq[l], mla_kv_norm[l], mla_w_ukv[l],
                          w_branch_a[l], w_branch_b[l], w_branch_c[l], w_out[l], ffn_norm[l],
                          w_ffn_gate[l], w_ffn_up[l], w_ffn_down[l])
    return _rms_norm(x, final_norm)
```

```python
import functools
import math

import jax
import jax.numpy as jnp
import numpy as np
from jax import lax
from jax.experimental import pallas as pl
from jax.experimental.pallas import tpu as pltpu

D_MODEL = 1024
SEQ = 2048
DEPTH = 2
A_HEADS = 4
A_HEAD_DIM = 64
B_HEADS = 4
B_HEAD_DIM = 128
B_PATTERNS = ((128, 1), (512, 4), (2048, 16))
C_HEADS = 8
C_NOPE_DIM = 64
C_ROPE_DIM = 32
C_V_DIM = 64
C_Q_LORA = 384
C_KV_LORA = 256
ROPE_THETA = 10000.0
FFN_HIDDEN = 2816
A_W = A_HEADS * 2 * A_HEAD_DIM
B_W = B_HEADS * B_HEAD_DIM
C_OUT_W = C_HEADS * C_V_DIM
C_Q_OFF = 2 * A_W + A_W + 3 * B_W
C_KV_OFF = C_Q_OFF + C_Q_LORA
C_PE_OFF = C_KV_OFF + C_KV_LORA
GATE_OFF = C_PE_OFF + C_ROPE_DIM
NORM_EPS = 1e-6
NEG_INF = -1e30
LOG2E = 1.4426950408889634

LANES = 128
T_ATT = 256
VMEM_LIMIT = 48 * 1024 * 1024

BF16 = jnp.bfloat16
F32 = jnp.float32


def _rms(xf, g):
    return xf * lax.rsqrt(jnp.mean(xf * xf, axis=-1, keepdims=True) + NORM_EPS) * g


def _nt_dot(a, b):
    return lax.dot_general(a, b, (((1,), (1,)), ((), ())), preferred_element_type=F32)


QKG_TN = 1024
QKG_W = 2 * A_W + 2 * B_W + 3 * D_MODEL
N_QK_BLOCKS = (2 * A_W + 2 * B_W) // QKG_TN


def _qkg_kernel(x_ref, g_ref, w_ref, cs_ref, o_ref, h_ref):
    j = pl.program_id(1)

    @pl.when(j == 0)
    def _():
        h_ref[...] = _rms(x_ref[...], g_ref[...]).astype(BF16)

    acc = jnp.dot(h_ref[...], w_ref[...], preferred_element_type=F32)

    @pl.when(j < N_QK_BLOCKS)
    def _():
        o_ref[...] = (acc * cs_ref[...]).astype(BF16)

    @pl.when(j >= N_QK_BLOCKS)
    def _():
        o_ref[...] = jax.nn.sigmoid(acc).astype(BF16)


def _qkg_proj(x2d, g, w, colscale, tm):
    m = x2d.shape[0]
    return pl.pallas_call(
        _qkg_kernel,
        grid=(m // tm, QKG_W // QKG_TN),
        in_specs=[
            pl.BlockSpec((tm, D_MODEL), lambda i, j: (i, 0)),
            pl.BlockSpec((1, D_MODEL), lambda i, j: (0, 0)),
            pl.BlockSpec((D_MODEL, QKG_TN), lambda i, j: (0, j)),
            pl.BlockSpec((1, QKG_TN), lambda i, j: (0, j)),
        ],
        out_specs=pl.BlockSpec((tm, QKG_TN), lambda i, j: (i, j)),
        out_shape=jax.ShapeDtypeStruct((m, QKG_W), BF16),
        scratch_shapes=[pltpu.VMEM((tm, D_MODEL), BF16)],
        compiler_params=pltpu.CompilerParams(
            dimension_semantics=("arbitrary", "arbitrary"), vmem_limit_bytes=VMEM_LIMIT),
        name="qkg_proj",
    )(x2d, g, w, colscale)


C_CW = 768
C_PE_GROUP = C_Q_LORA + C_KV_LORA


def _misc_kernel(x_ref, g_ref, wvt_ref, wc_ref, qn_ref, kvn_ref, wuq_ref, wuk_ref, wuvt_ref,
                 qtab_ref, ktab_ref, vta_ref, vtb_ref, qc_ref, kc_ref, vtc_ref):
    tm = x_ref.shape[0]
    h = _rms(x_ref[...], g_ref[...]).astype(BF16)

    vt = _nt_dot(wvt_ref[...], h).astype(BF16)
    for t in range(tm // T_ATT):
        vta_ref[t] = vt[:A_W, t * T_ATT:(t + 1) * T_ATT]
        vtb_ref[t] = vt[A_W:, t * T_ATT:(t + 1) * T_ATT]

    c = jnp.dot(h, wc_ref[...], preferred_element_type=F32)
    cqn = _rms(c[:, :C_Q_LORA], qn_ref[...]).astype(BF16)
    ckvn = _rms(c[:, C_Q_LORA:C_PE_GROUP], kvn_ref[...]).astype(BF16)

    q = jnp.dot(cqn, wuq_ref[...], preferred_element_type=F32)
    qtab = qtab_ref[...]
    for hd in range(C_HEADS):
        sl = slice(hd * LANES, (hd + 1) * LANES)
        qc_ref[:, sl] = (q[:, sl] * qtab).astype(BF16)

    kt = c[:, C_PE_GROUP:] * ktab_ref[...]
    lane = lax.broadcasted_iota(jnp.int32, kt.shape, 1)
    swapped = jnp.where(lane < 96, pltpu.roll(kt, 96, 1), pltpu.roll(kt, 32, 1))
    kp2 = jnp.where(lane >= 64, kt + swapped, 0.0)
    kk = jnp.dot(ckvn, wuk_ref[...], preferred_element_type=F32)
    for hd in range(C_HEADS):
        sl = slice(hd * LANES, (hd + 1) * LANES)
        kc_ref[:, sl] = (kk[:, sl] + kp2).astype(BF16)

    vtc = _nt_dot(wuvt_ref[...], ckvn).astype(BF16)
    for t in range(tm // T_ATT):
        vtc_ref[t] = vtc[:, t * T_ATT:(t + 1) * T_ATT]


def _misc_proj(x2d, g, wvt, wc, qn, kvn, wuq, wuk, wuvt, qtab, ktab, batch, tm):
    m = x2d.shape[0]
    nk = SEQ // T_ATT
    per_seq = SEQ // tm
    tiles = tm // T_ATT
    full = lambda shape: pl.BlockSpec(shape, lambda i: (0,) * len(shape))
    vt_spec = lambda w: pl.BlockSpec((None, tiles, w, T_ATT), lambda i: (i // per_seq, i % per_seq, 0, 0))
    return pl.pallas_call(
        _misc_kernel,
        grid=(m // tm,),
        in_specs=[
            pl.BlockSpec((tm, D_MODEL), lambda i: (i, 0)),
            full((1, D_MODEL)),
            full((A_W + B_W, D_MODEL)),
            full((D_MODEL, C_CW)),
            full((1, C_Q_LORA)),
            full((1, C_KV_LORA)),
            full((C_Q_LORA, C_HEADS * LANES)),
            full((C_KV_LORA, C_HEADS * LANES)),
            full((C_OUT_W, C_KV_LORA)),
            pl.BlockSpec((tm, LANES), lambda i: (i % per_seq, 0)),
            pl.BlockSpec((tm, LANES), lambda i: (i % per_seq, 0)),
        ],
        out_specs=[
            vt_spec(A_W), vt_spec(B_W),
            pl.BlockSpec((tm, C_HEADS * LANES), lambda i: (i, 0)),
            pl.BlockSpec((tm, C_HEADS * LANES), lambda i: (i, 0)),
            vt_spec(C_OUT_W),
        ],
        out_shape=[
            jax.ShapeDtypeStruct((batch, nk, A_W, T_ATT), BF16),
            jax.ShapeDtypeStruct((batch, nk, B_W, T_ATT), BF16),
            jax.ShapeDtypeStruct((m, C_HEADS * LANES), BF16),
            jax.ShapeDtypeStruct((m, C_HEADS * LANES), BF16),
            jax.ShapeDtypeStruct((batch, nk, C_OUT_W, T_ATT), BF16),
        ],
        compiler_params=pltpu.CompilerParams(
            dimension_semantics=("arbitrary",), vmem_limit_bytes=VMEM_LIMIT),
        name="misc_proj",
    )(x2d, g, wvt, wc, qn, kvn, wuq, wuk, wuvt, qtab, ktab)


def _attend(qs, k_tile, vt_tile, tab_ref, m_ref, l_ref, acc_ref):
    qt = pl.program_id(2)
    m_ref[...] = jnp.full(m_ref.shape, NEG_INF, F32)
    l_ref[...] = jnp.zeros(l_ref.shape, F32)
    acc_ref[...] = jnp.zeros(acc_ref.shape, F32)

    def body(kt, carry):
        bias = tab_ref[qt - kt]
        for s, q in enumerate(qs):
            st = _nt_dot(k_tile(s, kt), q) + bias
            m_old = m_ref[s]
            m_new = jnp.maximum(m_old, jnp.max(st, axis=0, keepdims=True))
            alpha = jnp.exp2(m_old - m_new)
            p = jnp.exp2(st - m_new)
            l_ref[s] = alpha * l_ref[s] + jnp.sum(p, axis=0, keepdims=True)
            pv = jnp.dot(vt_tile(s, kt), p.astype(BF16), preferred_element_type=F32)
            acc_ref[s] = alpha * acc_ref[s] + pv
            m_ref[s] = m_new
        return carry

    lax.fori_loop(0, qt + 1, body, 0)


def _key_rows(kt):
    return pl.ds(pl.multiple_of(kt * T_ATT, T_ATT), T_ATT)


def _diff_attn_kernel(lam_init, q_ref, k_ref, vt_ref, tab_ref, dl_ref, gain_ref, o_ref,
                      m_ref, l_ref, acc_ref):
    q = q_ref[...]
    lane = lax.broadcasted_iota(jnp.int32, q.shape, 1)
    zero = jnp.zeros_like(q)
    qs = [jnp.where(lane < A_HEAD_DIM, q, zero), jnp.where(lane >= A_HEAD_DIM, q, zero)]
    _attend(qs, lambda s, kt: k_ref[_key_rows(kt), :], lambda s, kt: vt_ref[kt],
            tab_ref, m_ref, l_ref, acc_ref)

    dl = dl_ref[...]
    lam = (jnp.exp(jnp.sum(dl[0:1] * dl[1:2], axis=-1, keepdims=True))
           - jnp.exp(jnp.sum(dl[2:3] * dl[3:4], axis=-1, keepdims=True)) + lam_init)
    o = acc_ref[0] / l_ref[0] - lam * (acc_ref[1] / l_ref[1])
    y = o * lax.rsqrt(jnp.mean(o * o, axis=0, keepdims=True) + NORM_EPS) * gain_ref[...]
    o_ref[...] = (y * (1.0 - lam_init)).T.astype(BF16)


def _single_attn_kernel(q_ref, k_ref, vt_ref, tab_ref, o_ref, m_ref, l_ref, acc_ref):
    _attend([q_ref[...]], lambda s, kt: k_ref[_key_rows(kt), :], lambda s, kt: vt_ref[kt],
            tab_ref, m_ref, l_ref, acc_ref)
    o_ref[...] = (acc_ref[0] / l_ref[0]).T.astype(BF16)


def _pair_attn_kernel(q_ref, k_ref, vt_ref, tab_ref, o_ref, m_ref, l_ref, acc_ref):
    qs = [q_ref[:, :LANES], q_ref[:, LANES:]]
    _attend(qs,
            lambda s, kt: k_ref[_key_rows(kt), s * LANES:(s + 1) * LANES],
            lambda s, kt: vt_ref[kt, s * C_V_DIM:(s + 1) * C_V_DIM, :],
            tab_ref, m_ref, l_ref, acc_ref)
    o = jnp.concatenate([acc_ref[0] / l_ref[0], acc_ref[1] / l_ref[1]], axis=0)
    o_ref[...] = o.T.astype(BF16)


def _flash_call(kernel, name, q_arr, q_col0, k_arr, k_col0, vt_arr, tab, extra, extra_specs,
                batch, heads, width, streams, dv, tab_per_head):
    nq = SEQ // T_ATT
    m = q_arr.shape[0]
    vrows = vt_arr.shape[2] // heads
    qb, kb = q_col0 // width, k_col0 // width
    in_specs = [
        pl.BlockSpec((T_ATT, width), lambda h, b, qt: (b * nq + qt, qb + h)),
        pl.BlockSpec((SEQ, width), lambda h, b, qt: (b, kb + h)),
        pl.BlockSpec((None, nq, vrows, T_ATT), lambda h, b, qt: (b, 0, h, 0)),
        pl.BlockSpec((None, nq, T_ATT, T_ATT),
                     (lambda h, b, qt: (h, 0, 0, 0)) if tab_per_head else (lambda h, b, qt: (0, 0, 0, 0))),
    ] + extra_specs
    return pl.pallas_call(
        kernel,
        grid=(heads, batch, nq),
        in_specs=in_specs,
        out_specs=pl.BlockSpec((T_ATT, LANES), lambda h, b, qt: (b * nq + qt, h)),
        out_shape=jax.ShapeDtypeStruct((m, heads * LANES), BF16),
        scratch_shapes=[
            pltpu.VMEM((streams, 1, T_ATT), F32),
            pltpu.VMEM((streams, 1, T_ATT), F32),
            pltpu.VMEM((streams, dv, T_ATT), F32),
        ],
        compiler_params=pltpu.CompilerParams(
            dimension_semantics=("arbitrary", "arbitrary", "arbitrary"), vmem_limit_bytes=VMEM_LIMIT),
        name=name,
    )(q_arr, k_arr, vt_arr, tab, *extra)


def _mix_kernel(x_ref, oa_ref, ob_ref, oc_ref, ga_ref, gb_ref, gc_ref, wa_ref, wb_ref, wc_ref,
                wo_ref, o_ref):
    ya = jnp.dot(oa_ref[...], wa_ref[...], preferred_element_type=F32)
    yb = jnp.dot(ob_ref[...], wb_ref[...], preferred_element_type=F32)
    yc = jnp.dot(oc_ref[...], wc_ref[...], preferred_element_type=F32)
    mix = (ga_ref[...].astype(F32) * ya + gb_ref[...].astype(F32) * yb
           + gc_ref[...].astype(F32) * yc)
    o_ref[...] = x_ref[...] + jnp.dot(mix.astype(BF16), wo_ref[...], preferred_element_type=F32)


def _mix(x2d, oa, ob, oc, qkg, wa, wb, wc, wo, tm):
    m = x2d.shape[0]
    g0 = (2 * A_W + 2 * B_W) // D_MODEL
    row = lambda w: pl.BlockSpec((tm, w), lambda i: (i, 0))
    gate = lambda n: pl.BlockSpec((tm, D_MODEL), lambda i: (i, g0 + n))
    full = lambda shape: pl.BlockSpec(shape, lambda i: (0, 0))
    return pl.pallas_call(
        _mix_kernel,
        grid=(m // tm,),
        in_specs=[row(D_MODEL), row(A_W), row(B_W), row(C_OUT_W), gate(0), gate(1), gate(2),
                  full((A_W, D_MODEL)), full((B_W, D_MODEL)), full((C_OUT_W, D_MODEL)),
                  full((D_MODEL, D_MODEL))],
        out_specs=row(D_MODEL),
        out_shape=jax.ShapeDtypeStruct((m, D_MODEL), F32),
        compiler_params=pltpu.CompilerParams(
            dimension_semantics=("arbitrary",), vmem_limit_bytes=VMEM_LIMIT),
        name="mix",
    )(x2d, oa, ob, oc, qkg, qkg, qkg, wa, wb, wc, wo)


FFN_TH = FFN_HIDDEN // 2


def _ffn_kernel(final, x_ref, g_ref, wg_ref, wu_ref, wd_ref, fg_ref, o_ref, h_ref):
    j = pl.program_id(1)

    @pl.when(j == 0)
    def _():
        h_ref[...] = _rms(x_ref[...], g_ref[...]).astype(BF16)

    h = h_ref[...]
    gate = jnp.dot(h, wg_ref[...], preferred_element_type=F32)
    up = jnp.dot(h, wu_ref[...], preferred_element_type=F32)
    act = (gate * jax.nn.sigmoid(gate) * up).astype(BF16)
    part = jnp.dot(act, wd_ref[...], preferred_element_type=F32)

    @pl.when(j == 0)
    def _():
        o_ref[...] = x_ref[...] + part

    @pl.when(j > 0)
    def _():
        o_ref[...] += part

    if final:
        @pl.when(j == pl.num_programs(1) - 1)
        def _():
            o_ref[...] = _rms(o_ref[...], fg_ref[...])


def _ffn(x2d, g, wg, wu, wd, fg, final, tm):
    m = x2d.shape[0]
    return pl.pallas_call(
        functools.partial(_ffn_kernel, final),
        grid=(m // tm, FFN_HIDDEN // FFN_TH),
        in_specs=[
            pl.BlockSpec((tm, D_MODEL), lambda i, j: (i, 0)),
            pl.BlockSpec((1, D_MODEL), lambda i, j: (0, 0)),
            pl.BlockSpec((D_MODEL, FFN_TH), lambda i, j: (0, j)),
            pl.BlockSpec((D_MODEL, FFN_TH), lambda i, j: (0, j)),
            pl.BlockSpec((FFN_TH, D_MODEL), lambda i, j: (j, 0)),
            pl.BlockSpec((1, D_MODEL), lambda i, j: (0, 0)),
        ],
        out_specs=pl.BlockSpec((tm, D_MODEL), lambda i, j: (i, 0)),
        out_shape=jax.ShapeDtypeStruct((m, D_MODEL), F32),
        scratch_shapes=[pltpu.VMEM((tm, D_MODEL), BF16)],
        compiler_params=pltpu.CompilerParams(
            dimension_semantics=("arbitrary", "arbitrary"), vmem_limit_bytes=VMEM_LIMIT),
        name="ffn",
    )(x2d, g, wg, wu, wd, fg)


def _alibi_slopes():
    n = A_HEADS + B_HEADS
    s = 2.0 ** (-8.0 * np.arange(1, n + 1, dtype=np.float64) / n)
    return s[0::2], s[1::2]


def _bias_tables(slopes, multiplicity):
    nq = SEQ // T_ATT
    delta = jnp.arange(nq, dtype=jnp.int32)[:, None, None]
    j = jnp.arange(T_ATT, dtype=jnp.int32)[None, :, None]
    i = jnp.arange(T_ATT, dtype=jnp.int32)[None, None, :]
    d = delta * T_ATT + i - j
    mult = multiplicity(d)
    valid = (d >= 0) & (mult > 0)
    logm = jnp.log2(jnp.maximum(mult, 1).astype(F32))
    sl = jnp.asarray(np.asarray(slopes) * LOG2E, F32)[:, None, None, None]
    tab = logm[None] - sl * d.astype(F32)[None]
    return jnp.where(valid[None], tab, NEG_INF)


def _dilated_multiplicity(d):
    mult = jnp.zeros_like(d)
    for window, dil in B_PATTERNS:
        mult = mult + ((d % dil == 0) & (d // dil <= window // dil)).astype(jnp.int32)
    return mult


def _rope_tables():
    half = C_ROPE_DIM // 2
    inv_freq = ROPE_THETA ** (-jnp.arange(half, dtype=F32) / half)
    ang = jnp.arange(SEQ, dtype=F32)[:, None] * inv_freq[None, :]
    cos = jnp.concatenate([jnp.cos(ang)] * 2, axis=-1)
    sin = jnp.concatenate([jnp.sin(ang)] * 2, axis=-1)
    scale = (C_NOPE_DIM + C_ROPE_DIM) ** -0.5 * LOG2E
    qtab = jnp.concatenate([jnp.full((SEQ, C_NOPE_DIM), scale, F32), cos * scale, sin * scale], axis=-1)
    ktab = jnp.concatenate([jnp.zeros((SEQ, C_NOPE_DIM), F32), cos, sin], axis=-1)
    return qtab, ktab


def _rotate_half_cols(w):
    half = C_ROPE_DIM // 2
    return jnp.concatenate([-w[..., half:], w[..., :half]], axis=-1)


def _layer_weights(w_in, w_uq, w_ukv):
    sa, sb = A_HEAD_DIM ** -0.5 * LOG2E, B_HEAD_DIM ** -0.5 * LOG2E
    b0 = 3 * A_W
    w_qkg = jnp.concatenate(
        [w_in[:, :2 * A_W], w_in[:, b0:b0 + 2 * B_W], w_in[:, GATE_OFF:]], axis=1).astype(BF16)
    colscale = jnp.concatenate(
        [jnp.full((A_W,), sa, F32), jnp.ones((A_W,), F32), jnp.full((B_W,), sb, F32),
         jnp.ones((B_W,), F32), jnp.ones((3 * D_MODEL,), F32)])[None, :]
    w_vt = jnp.concatenate([w_in[:, 2 * A_W:3 * A_W], w_in[:, b0 + 2 * B_W:b0 + 3 * B_W]], axis=1).T.astype(BF16)
    w_pe = w_in[:, C_PE_OFF:GATE_OFF]
    w_c = jnp.concatenate(
        [w_in[:, C_Q_OFF:C_PE_OFF], jnp.zeros((D_MODEL, C_NOPE_DIM), F32), w_pe, _rotate_half_cols(w_pe)],
        axis=1).astype(BF16)
    uq = w_uq.reshape(C_Q_LORA, C_HEADS, C_NOPE_DIM + C_ROPE_DIM)
    uq_pe = uq[..., C_NOPE_DIM:]
    w_uq_ext = jnp.concatenate([uq, _rotate_half_cols(uq_pe)], axis=-1).reshape(C_Q_LORA, C_HEADS * LANES).astype(BF16)
    ukv = w_ukv.reshape(C_KV_LORA, C_HEADS, C_NOPE_DIM + C_V_DIM)
    w_uk = jnp.concatenate([ukv[..., :C_NOPE_DIM], jnp.zeros((C_KV_LORA, C_HEADS, LANES - C_NOPE_DIM), F32)],
                           axis=-1).reshape(C_KV_LORA, C_HEADS * LANES).astype(BF16)
    w_uvt = ukv[..., C_NOPE_DIM:].reshape(C_KV_LORA, C_OUT_W).T.astype(BF16)
    return w_qkg, colscale, w_vt, w_c, w_uq_ext, w_uk, w_uvt


def kernel(x, attn_norm, w_in, diff_lambda, diff_norm, mla_q_norm, mla_w_uq, mla_kv_norm, mla_w_ukv,
           w_branch_a, w_branch_b, w_branch_c, w_out, ffn_norm, w_ffn_gate, w_ffn_up, w_ffn_down,
           final_norm):
    batch, seq, d_model = x.shape
    assert (seq, d_model) == (SEQ, D_MODEL)
    m = batch * seq
    tm_proj = 1024
    tm_tok = 512

    slopes_a, slopes_b = _alibi_slopes()
    tab_a = _bias_tables(slopes_a, lambda d: jnp.ones_like(d))
    tab_b = _bias_tables(slopes_b, _dilated_multiplicity)
    tab_c = _bias_tables(np.zeros((1,)), lambda d: jnp.ones_like(d))
    qtab, ktab = _rope_tables()

    x2d = x.reshape(m, D_MODEL)
    for l in range(DEPTH):
        w_qkg, colscale, w_vt, w_c, w_uq_ext, w_uk, w_uvt = _layer_weights(w_in[l], mla_w_uq[l], mla_w_ukv[l])
        g_attn = attn_norm[l][None, :]
        qkg = _qkg_proj(x2d, g_attn, w_qkg, colscale, tm_proj)
        vta, vtb, qc, kc, vtc = _misc_proj(
            x2d, g_attn, w_vt, w_c, mla_q_norm[l][None, :], mla_kv_norm[l][None, :], w_uq_ext, w_uk, w_uvt,
            qtab, ktab, batch, tm_tok)

        lam_init = 0.8 - 0.6 * math.exp(-0.3 * l)
        oa = _flash_call(
            functools.partial(_diff_attn_kernel, lam_init), "diff_attn", qkg, 0, qkg, A_W, vta, tab_a,
            [diff_lambda[l], diff_norm[l][:, None]],
            [pl.BlockSpec((4, A_HEAD_DIM), lambda h, b, qt: (0, 0)),
             pl.BlockSpec((2 * A_HEAD_DIM, 1), lambda h, b, qt: (0, 0))],
            batch, A_HEADS, LANES, 2, 2 * A_HEAD_DIM, True)
        ob = _flash_call(_single_attn_kernel, "dilated_attn", qkg, 2 * A_W, qkg, 2 * A_W + B_W, vtb, tab_b,
                         [], [], batch, B_HEADS, LANES, 1, B_HEAD_DIM, True)
        oc = _flash_call(_pair_attn_kernel, "mla_attn", qc, 0, kc, 0, vtc, tab_c,
                         [], [], batch, C_HEADS // 2, 2 * LANES, 2, C_V_DIM, False)

        x2d = _mix(x2d, oa, ob, oc, qkg, w_branch_a[l].astype(BF16), w_branch_b[l].astype(BF16),
                   w_branch_c[l].astype(BF16), w_out[l].astype(BF16), tm_tok)
        x2d = _ffn(x2d, ffn_norm[l][None, :], w_ffn_gate[l].astype(BF16), w_ffn_up[l].astype(BF16),
                   w_ffn_down[l].astype(BF16), final_norm[None, :], l == DEPTH - 1, tm_tok)
    return x2d.reshape(batch, seq, d_model)
```

```python
import functools
import math

import jax
import jax.numpy as jnp
import numpy as np
from jax import lax
from jax.experimental import pallas as pl
from jax.experimental.pallas import tpu as pltpu

D_MODEL = 1024
SEQ = 2048
DEPTH = 2
A_HEADS = 4
A_HEAD_DIM = 64
B_HEADS = 4
B_HEAD_DIM = 128
B_PATTERNS = ((128, 1), (512, 4), (2048, 16))
C_HEADS = 8
C_NOPE_DIM = 64
C_ROPE_DIM = 32
C_V_DIM = 64
C_Q_LORA = 384
C_KV_LORA = 256
ROPE_THETA = 10000.0
FFN_HIDDEN = 2816
A_W = A_HEADS * 2 * A_HEAD_DIM
B_W = B_HEADS * B_HEAD_DIM
C_OUT_W = C_HEADS * C_V_DIM
C_Q_OFF = 2 * A_W + A_W + 3 * B_W
C_KV_OFF = C_Q_OFF + C_Q_LORA
C_PE_OFF = C_KV_OFF + C_KV_LORA
GATE_OFF = C_PE_OFF + C_ROPE_DIM
NORM_EPS = 1e-6
NEG_INF = -1e30
LOG2E = 1.4426950408889634

LANES = 128
T_ATT = 256
VMEM_LIMIT = 48 * 1024 * 1024

BF16 = jnp.bfloat16
F32 = jnp.float32


def _rms(xf, g):
    return xf * lax.rsqrt(jnp.mean(xf * xf, axis=-1, keepdims=True) + NORM_EPS) * g


def _nt_dot(a, b):
    return lax.dot_general(a, b, (((1,), (1,)), ((), ())), preferred_element_type=F32)


QKG_TN = 1024
QKG_W = 2 * A_W + 2 * B_W + 3 * D_MODEL
N_QK_BLOCKS = (2 * A_W + 2 * B_W) // QKG_TN


def _qkg_kernel(x_ref, g_ref, w_ref, cs_ref, o_ref, h_ref):
    j = pl.program_id(1)

    @pl.when(j == 0)
    def _():
        h_ref[...] = _rms(x_ref[...], g_ref[...]).astype(BF16)

    acc = jnp.dot(h_ref[...], w_ref[...], preferred_element_type=F32)

    @pl.when(j < N_QK_BLOCKS)
    def _():
        o_ref[...] = (acc * cs_ref[...]).astype(BF16)

    @pl.when(j >= N_QK_BLOCKS)
    def _():
        o_ref[...] = jax.nn.sigmoid(acc).astype(BF16)


def _qkg_proj(x2d, g, w, colscale, tm):
    m = x2d.shape[0]
    return pl.pallas_call(
        _qkg_kernel,
        grid=(m // tm, QKG_W // QKG_TN),
        in_specs=[
            pl.BlockSpec((tm, D_MODEL), lambda i, j: (i, 0)),
            pl.BlockSpec((1, D_MODEL), lambda i, j: (0, 0)),
            pl.BlockSpec((D_MODEL, QKG_TN), lambda i, j: (0, j)),
            pl.BlockSpec((1, QKG_TN), lambda i, j: (0, j)),
        ],
        out_specs=pl.BlockSpec((tm, QKG_TN), lambda i, j: (i, j)),
        out_shape=jax.ShapeDtypeStruct((m, QKG_W), BF16),
        scratch_shapes=[pltpu.VMEM((tm, D_MODEL), BF16)],
        compiler_params=pltpu.CompilerParams(
            dimension_semantics=("arbitrary", "arbitrary"), vmem_limit_bytes=VMEM_LIMIT),
        name="qkg_proj",
    )(x2d, g, w, colscale)


C_CW = 768
C_PE_GROUP = C_Q_LORA + C_KV_LORA


def _misc_kernel(x_ref, g_ref, wvt_ref, wc_ref, qn_ref, kvn_ref, wuq_ref, wuk_ref, wuvt_ref,
                 qtab_ref, ktab_ref, vta_ref, vtb_ref, qc_ref, kc_ref, vtc_ref):
    tm = x_ref.shape[0]
    h = _rms(x_ref[...], g_ref[...]).astype(BF16)

    vt = _nt_dot(wvt_ref[...], h).astype(BF16)
    vta_ref[...] = vt[:A_W]
    vtb_ref[...] = vt[A_W:]

    c = jnp.dot(h, wc_ref[...], preferred_element_type=F32)
    cqn = _rms(c[:, :C_Q_LORA], qn_ref[...]).astype(BF16)
    ckvn = _rms(c[:, C_Q_LORA:C_PE_GROUP], kvn_ref[...]).astype(BF16)

    q = jnp.dot(cqn, wuq_ref[...], preferred_element_type=F32)
    qtab = qtab_ref[...]
    for hd in range(C_HEADS):
        sl = slice(hd * LANES, (hd + 1) * LANES)
        qc_ref[:, sl] = (q[:, sl] * qtab).astype(BF16)

    kt = c[:, C_PE_GROUP:] * ktab_ref[...]
    lane = lax.broadcasted_iota(jnp.int32, kt.shape, 1)
    swapped = jnp.where(lane < 96, pltpu.roll(kt, 96, 1), pltpu.roll(kt, 32, 1))
    kp2 = jnp.where(lane >= 64, kt + swapped, 0.0)
    kk = jnp.dot(ckvn, wuk_ref[...], preferred_element_type=F32)
    for hd in range(C_HEADS):
        sl = slice(hd * LANES, (hd + 1) * LANES)
        kc_ref[:, sl] = (kk[:, sl] + kp2).astype(BF16)

    vtc_ref[...] = _nt_dot(wuvt_ref[...], ckvn).astype(BF16)


def _misc_proj(x2d, g, wvt, wc, qn, kvn, wuq, wuk, wuvt, qtab, ktab, batch, tm):
    m = x2d.shape[0]
    per_seq = SEQ // tm
    full = lambda shape: pl.BlockSpec(shape, lambda i: (0,) * len(shape))
    vt_spec = lambda w: pl.BlockSpec((None, w, tm), lambda i: (i // per_seq, 0, i % per_seq))
    return pl.pallas_call(
        _misc_kernel,
        grid=(m // tm,),
        in_specs=[
            pl.BlockSpec((tm, D_MODEL), lambda i: (i, 0)),
            full((1, D_MODEL)),
            full((A_W + B_W, D_MODEL)),
            full((D_MODEL, C_CW)),
            full((1, C_Q_LORA)),
            full((1, C_KV_LORA)),
            full((C_Q_LORA, C_HEADS * LANES)),
            full((C_KV_LORA, C_HEADS * LANES)),
            full((C_OUT_W, C_KV_LORA)),
            pl.BlockSpec((tm, LANES), lambda i: (i % per_seq, 0)),
            pl.BlockSpec((tm, LANES), lambda i: (i % per_seq, 0)),
        ],
        out_specs=[
            vt_spec(A_W), vt_spec(B_W),
            pl.BlockSpec((tm, C_HEADS * LANES), lambda i: (i, 0)),
            pl.BlockSpec((tm, C_HEADS * LANES), lambda i: (i, 0)),
            vt_spec(C_OUT_W),
        ],
        out_shape=[
            jax.ShapeDtypeStruct((batch, A_W, SEQ), BF16),
            jax.ShapeDtypeStruct((batch, B_W, SEQ), BF16),
            jax.ShapeDtypeStruct((m, C_HEADS * LANES), BF16),
            jax.ShapeDtypeStruct((m, C_HEADS * LANES), BF16),
            jax.ShapeDtypeStruct((batch, C_OUT_W, SEQ), BF16),
        ],
        compiler_params=pltpu.CompilerParams(
            dimension_semantics=("arbitrary",), vmem_limit_bytes=VMEM_LIMIT),
        name="misc_proj",
    )(x2d, g, wvt, wc, qn, kvn, wuq, wuk, wuvt, qtab, ktab)


NQ_ATT = SEQ // T_ATT


def _key_tile(jt):
    return slice(jt * T_ATT, (jt + 1) * T_ATT)


def _attend_tiles(qi, qs, k_tile, vt_keys, tab_ref, bias_off_diag, st_ref, p_ref, acc_ref, l_ref):
    n_tiles = qi + 1
    r0 = tab_ref.shape[0] - n_tiles * T_ATT
    for s, q in enumerate(qs):
        m = None
        for jt in range(n_tiles):
            st = _nt_dot(k_tile(s, jt), q)
            if bias_off_diag or jt == qi:
                st = st + tab_ref[r0 + jt * T_ATT:r0 + (jt + 1) * T_ATT, :]
            st_ref[s, _key_tile(jt), :] = st
            mt = jnp.max(st, axis=0, keepdims=True)
            m = mt if m is None else jnp.maximum(m, mt)
        l = None
        for jt in range(n_tiles):
            p = jnp.exp2(st_ref[s, _key_tile(jt), :] - m)
            lt = jnp.sum(p, axis=0, keepdims=True)
            l = lt if l is None else l + lt
            p_ref[s, _key_tile(jt), :] = p.astype(BF16)
        n = n_tiles * T_ATT
        acc_ref[s] = jnp.dot(vt_keys(s, n), p_ref[s, :n, :], preferred_element_type=F32)
        l_ref[s] = l


def _attend(qs, k_tile, vt_keys, tab_ref, bias_off_diag, st_ref, p_ref, acc_ref, l_ref):
    qt = pl.program_id(2)
    for qi in range(NQ_ATT):
        @pl.when(qt == qi)
        def _():
            _attend_tiles(qi, qs, k_tile, vt_keys, tab_ref, bias_off_diag, st_ref, p_ref, acc_ref, l_ref)


def _diff_attn_kernel(lam_init, q_ref, k_ref, vt_ref, tab_ref, dl_ref, gain_ref, o_ref,
                      st_ref, p_ref, acc_ref, l_ref):
    q = q_ref[...]
    lane = lax.broadcasted_iota(jnp.int32, q.shape, 1)
    zero = jnp.zeros_like(q)
    qs = [jnp.where(lane < A_HEAD_DIM, q, zero), jnp.where(lane >= A_HEAD_DIM, q, zero)]
    _attend(qs, lambda s, jt: k_ref[_key_tile(jt), :], lambda s, n: vt_ref[:, :n],
            tab_ref, True, st_ref, p_ref, acc_ref, l_ref)

    dl = dl_ref[...]
    lam = (jnp.exp(jnp.sum(dl[0:1] * dl[1:2], axis=-1, keepdims=True))
           - jnp.exp(jnp.sum(dl[2:3] * dl[3:4], axis=-1, keepdims=True)) + lam_init)
    o = acc_ref[0] / l_ref[0] - lam * (acc_ref[1] / l_ref[1])
    y = o * lax.rsqrt(jnp.mean(o * o, axis=0, keepdims=True) + NORM_EPS) * gain_ref[...]
    o_ref[...] = (y * (1.0 - lam_init)).T.astype(BF16)


def _single_attn_kernel(q_ref, k_ref, vt_ref, tab_ref, o_ref, st_ref, p_ref, acc_ref, l_ref):
    _attend([q_ref[...]], lambda s, jt: k_ref[_key_tile(jt), :], lambda s, n: vt_ref[:, :n],
            tab_ref, True, st_ref, p_ref, acc_ref, l_ref)
    o_ref[...] = (acc_ref[0] / l_ref[0]).T.astype(BF16)


def _pair_attn_kernel(q_ref, k_ref, vt_ref, tab_ref, o_ref, st_ref, p_ref, acc_ref, l_ref):
    qs = [q_ref[:, :LANES], q_ref[:, LANES:]]
    _attend(qs,
            lambda s, jt: k_ref[_key_tile(jt), s * LANES:(s + 1) * LANES],
            lambda s, n: vt_ref[s * C_V_DIM:(s + 1) * C_V_DIM, :n],
            tab_ref, False, st_ref, p_ref, acc_ref, l_ref)
    o = jnp.concatenate([acc_ref[0] / l_ref[0], acc_ref[1] / l_ref[1]], axis=0)
    o_ref[...] = o.T.astype(BF16)


def _flash_call(kernel, name, q_arr, q_col0, k_arr, k_col0, vt_arr, tab, extra, extra_specs,
                batch, heads, width, streams, dv, tab_per_head):
    nq = NQ_ATT
    m = q_arr.shape[0]
    vrows = vt_arr.shape[1] // heads
    qb, kb = q_col0 // width, k_col0 // width
    in_specs = [
        pl.BlockSpec((T_ATT, width), lambda h, b, qt: (b * nq + qt, qb + h)),
        pl.BlockSpec((SEQ, width), lambda h, b, qt: (b, kb + h)),
        pl.BlockSpec((None, vrows, SEQ), lambda h, b, qt: (b, h, 0)),
        pl.BlockSpec((None,) + tab.shape[1:],
                     (lambda h, b, qt: (h, 0, 0)) if tab_per_head else (lambda h, b, qt: (0, 0, 0))),
    ] + extra_specs
    return pl.pallas_call(
        kernel,
        grid=(heads, batch, nq),
        in_specs=in_specs,
        out_specs=pl.BlockSpec((T_ATT, LANES), lambda h, b, qt: (b * nq + qt, h)),
        out_shape=jax.ShapeDtypeStruct((m, heads * LANES), BF16),
        scratch_shapes=[
            pltpu.VMEM((streams, SEQ, T_ATT), F32),
            pltpu.VMEM((streams, SEQ, T_ATT), BF16),
            pltpu.VMEM((streams, dv, T_ATT), F32),
            pltpu.VMEM((streams, 1, T_ATT), F32),
        ],
        compiler_params=pltpu.CompilerParams(
            dimension_semantics=("arbitrary", "arbitrary", "arbitrary"), vmem_limit_bytes=VMEM_LIMIT),
        name=name,
    )(q_arr, k_arr, vt_arr, tab, *extra)


def _mix_kernel(x_ref, oa_ref, ob_ref, oc_ref, ga_ref, gb_ref, gc_ref, wa_ref, wb_ref, wc_ref,
                wo_ref, o_ref):
    ya = jnp.dot(oa_ref[...], wa_ref[...], preferred_element_type=F32)
    yb = jnp.dot(ob_ref[...], wb_ref[...], preferred_element_type=F32)
    yc = jnp.dot(oc_ref[...], wc_ref[...], preferred_element_type=F32)
    mix = (ga_ref[...].astype(F32) * ya + gb_ref[...].astype(F32) * yb
           + gc_ref[...].astype(F32) * yc)
    o_ref[...] = x_ref[...] + jnp.dot(mix.astype(BF16), wo_ref[...], preferred_element_type=F32)


def _mix(x2d, oa, ob, oc, qkg, wa, wb, wc, wo, tm):
    m = x2d.shape[0]
    g0 = (2 * A_W + 2 * B_W) // D_MODEL
    row = lambda w: pl.BlockSpec((tm, w), lambda i: (i, 0))
    gate = lambda n: pl.BlockSpec((tm, D_MODEL), lambda i: (i, g0 + n))
    full = lambda shape: pl.BlockSpec(shape, lambda i: (0, 0))
    return pl.pallas_call(
        _mix_kernel,
        grid=(m // tm,),
        in_specs=[row(D_MODEL), row(A_W), row(B_W), row(C_OUT_W), gate(0), gate(1), gate(2),
                  full((A_W, D_MODEL)), full((B_W, D_MODEL)), full((C_OUT_W, D_MODEL)),
                  full((D_MODEL, D_MODEL))],
        out_specs=row(D_MODEL),
        out_shape=jax.ShapeDtypeStruct((m, D_MODEL), F32),
        compiler_params=pltpu.CompilerParams(
            dimension_semantics=("arbitrary",), vmem_limit_bytes=VMEM_LIMIT),
        name="mix",
    )(x2d, oa, ob, oc, qkg, qkg, qkg, wa, wb, wc, wo)


FFN_TH = FFN_HIDDEN // 2


def _ffn_kernel(final, x_ref, g_ref, wg_ref, wu_ref, wd_ref, fg_ref, o_ref, h_ref):
    j = pl.program_id(1)

    @pl.when(j == 0)
    def _():
        h_ref[...] = _rms(x_ref[...], g_ref[...]).astype(BF16)

    h = h_ref[...]
    gate = jnp.dot(h, wg_ref[...], preferred_element_type=F32)
    up = jnp.dot(h, wu_ref[...], preferred_element_type=F32)
    act = (gate * jax.nn.sigmoid(gate) * up).astype(BF16)
    part = jnp.dot(act, wd_ref[...], preferred_element_type=F32)

    @pl.when(j == 0)
    def _():
        o_ref[...] = x_ref[...] + part

    @pl.when(j > 0)
    def _():
        o_ref[...] += part

    if final:
        @pl.when(j == pl.num_programs(1) - 1)
        def _():
            o_ref[...] = _rms(o_ref[...], fg_ref[...])


def _ffn(x2d, g, wg, wu, wd, fg, final, tm):
    m = x2d.shape[0]
    return pl.pallas_call(
        functools.partial(_ffn_kernel, final),
        grid=(m // tm, FFN_HIDDEN // FFN_TH),
        in_specs=[
            pl.BlockSpec((tm, D_MODEL), lambda i, j: (i, 0)),
            pl.BlockSpec((1, D_MODEL), lambda i, j: (0, 0)),
            pl.BlockSpec((D_MODEL, FFN_TH), lambda i, j: (0, j)),
            pl.BlockSpec((D_MODEL, FFN_TH), lambda i, j: (0, j)),
            pl.BlockSpec((FFN_TH, D_MODEL), lambda i, j: (j, 0)),
            pl.BlockSpec((1, D_MODEL), lambda i, j: (0, 0)),
        ],
        out_specs=pl.BlockSpec((tm, D_MODEL), lambda i, j: (i, 0)),
        out_shape=jax.ShapeDtypeStruct((m, D_MODEL), F32),
        scratch_shapes=[pltpu.VMEM((tm, D_MODEL), BF16)],
        compiler_params=pltpu.CompilerParams(
            dimension_semantics=("arbitrary", "arbitrary"), vmem_limit_bytes=VMEM_LIMIT),
        name="ffn",
    )(x2d, g, wg, wu, wd, fg)


def _alibi_slopes():
    n = A_HEADS + B_HEADS
    s = 2.0 ** (-8.0 * np.arange(1, n + 1, dtype=np.float64) / n)
    return s[0::2], s[1::2]


def _bias_tables(slopes, multiplicity, rows):
    r = jnp.arange(rows, dtype=jnp.int32)[:, None]
    i = jnp.arange(T_ATT, dtype=jnp.int32)[None, :]
    d = (rows - T_ATT) - r + i
    mult = multiplicity(d)
    valid = (d >= 0) & (mult > 0)
    logm = jnp.log2(jnp.maximum(mult, 1).astype(F32))
    sl = jnp.asarray(np.asarray(slopes) * LOG2E, F32)[:, None, None]
    tab = logm[None] - sl * d.astype(F32)[None]
    return jnp.where(valid[None], tab, NEG_INF)


def _dilated_multiplicity(d):
    mult = jnp.zeros_like(d)
    for window, dil in B_PATTERNS:
        mult = mult + ((d % dil == 0) & (d // dil <= window // dil)).astype(jnp.int32)
    return mult


def _rope_tables():
    half = C_ROPE_DIM // 2
    inv_freq = ROPE_THETA ** (-jnp.arange(half, dtype=F32) / half)
    ang = jnp.arange(SEQ, dtype=F32)[:, None] * inv_freq[None, :]
    cos = jnp.concatenate([jnp.cos(ang)] * 2, axis=-1)
    sin = jnp.concatenate([jnp.sin(ang)] * 2, axis=-1)
    scale = (C_NOPE_DIM + C_ROPE_DIM) ** -0.5 * LOG2E
    qtab = jnp.concatenate([jnp.full((SEQ, C_NOPE_DIM), scale, F32), cos * scale, sin * scale], axis=-1)
    ktab = jnp.concatenate([jnp.zeros((SEQ, C_NOPE_DIM), F32), cos, sin], axis=-1)
    return qtab, ktab


def _rotate_half_cols(w):
    half = C_ROPE_DIM // 2
    return jnp.concatenate([-w[..., half:], w[..., :half]], axis=-1)


def _layer_weights(w_in, w_uq, w_ukv):
    sa, sb = A_HEAD_DIM ** -0.5 * LOG2E, B_HEAD_DIM ** -0.5 * LOG2E
    b0 = 3 * A_W
    w_qkg = jnp.concatenate(
        [w_in[:, :2 * A_W], w_in[:, b0:b0 + 2 * B_W], w_in[:, GATE_OFF:]], axis=1).astype(BF16)
    colscale = jnp.concatenate(
        [jnp.full((A_W,), sa, F32), jnp.ones((A_W,), F32), jnp.full((B_W,), sb, F32),
         jnp.ones((B_W,), F32), jnp.ones((3 * D_MODEL,), F32)])[None, :]
    w_vt = jnp.concatenate([w_in[:, 2 * A_W:3 * A_W], w_in[:, b0 + 2 * B_W:b0 + 3 * B_W]], axis=1).T.astype(BF16)
    w_pe = w_in[:, C_PE_OFF:GATE_OFF]
    w_c = jnp.concatenate(
        [w_in[:, C_Q_OFF:C_PE_OFF], jnp.zeros((D_MODEL, C_NOPE_DIM), F32), w_pe, _rotate_half_cols(w_pe)],
        axis=1).astype(BF16)
    uq = w_uq.reshape(C_Q_LORA, C_HEADS, C_NOPE_DIM + C_ROPE_DIM)
    uq_pe = uq[..., C_NOPE_DIM:]
    w_uq_ext = jnp.concatenate([uq, _rotate_half_cols(uq_pe)], axis=-1).reshape(C_Q_LORA, C_HEADS * LANES).astype(BF16)
    ukv = w_ukv.reshape(C_KV_LORA, C_HEADS, C_NOPE_DIM + C_V_DIM)
    w_uk = jnp.concatenate([ukv[..., :C_NOPE_DIM], jnp.zeros((C_KV_LORA, C_HEADS, LANES - C_NOPE_DIM), F32)],
                           axis=-1).reshape(C_KV_LORA, C_HEADS * LANES).astype(BF16)
    w_uvt = ukv[..., C_NOPE_DIM:].reshape(C_KV_LORA, C_OUT_W).T.astype(BF16)
    return w_qkg, colscale, w_vt, w_c, w_uq_ext, w_uk, w_uvt


def kernel(x, attn_norm, w_in, diff_lambda, diff_norm, mla_q_norm, mla_w_uq, mla_kv_norm, mla_w_ukv,
           w_branch_a, w_branch_b, w_branch_c, w_out, ffn_norm, w_ffn_gate, w_ffn_up, w_ffn_down,
           final_norm):
    batch, seq, d_model = x.shape
    assert (seq, d_model) == (SEQ, D_MODEL)
    m = batch * seq
    tm_proj = 1024
    tm_tok = 512

    slopes_a, slopes_b = _alibi_slopes()
    tab_a = _bias_tables(slopes_a, lambda d: jnp.ones_like(d), SEQ)
    tab_b = _bias_tables(slopes_b, _dilated_multiplicity, SEQ)
    tab_c = _bias_tables(np.zeros((1,)), lambda d: jnp.ones_like(d), T_ATT)
    qtab, ktab = _rope_tables()

    x2d = x.reshape(m, D_MODEL)
    for l in range(DEPTH):
        w_qkg, colscale, w_vt, w_c, w_uq_ext, w_uk, w_uvt = _layer_weights(w_in[l], mla_w_uq[l], mla_w_ukv[l])
        g_attn = attn_norm[l][None, :]
        qkg = _qkg_proj(x2d, g_attn, w_qkg, colscale, tm_proj)
        vta, vtb, qc, kc, vtc = _misc_proj(
            x2d, g_attn, w_vt, w_c, mla_q_norm[l][None, :], mla_kv_norm[l][None, :], w_uq_ext, w_uk, w_uvt,
            qtab, ktab, batch, tm_tok)

        lam_init = 0.8 - 0.6 * math.exp(-0.3 * l)
        oa = _flash_call(
            functools.partial(_diff_attn_kernel, lam_init), "diff_attn", qkg, 0, qkg, A_W, vta, tab_a,
            [diff_lambda[l], diff_norm[l][:, None]],
            [pl.BlockSpec((4, A_HEAD_DIM), lambda h, b, qt: (0, 0)),
             pl.BlockSpec((2 * A_HEAD_DIM, 1), lambda h, b, qt: (0, 0))],
            batch, A_HEADS, LANES, 2, 2 * A_HEAD_DIM, True)
        ob = _flash_call(_single_attn_kernel, "dilated_attn", qkg, 2 * A_W, qkg, 2 * A_W + B_W, vtb, tab_b,
                         [], [], batch, B_HEADS, LANES, 1, B_HEAD_DIM, True)
        oc = _flash_call(_pair_attn_kernel, "mla_attn", qc, 0, kc, 0, vtc, tab_c,
                         [], [], batch, C_HEADS // 2, 2 * LANES, 2, C_V_DIM, False)

        x2d = _mix(x2d, oa, ob, oc, qkg, w_branch_a[l].astype(BF16), w_branch_b[l].astype(BF16),
                   w_branch_c[l].astype(BF16), w_out[l].astype(BF16), tm_tok)
        x2d = _ffn(x2d, ffn_norm[l][None, :], w_ffn_gate[l].astype(BF16), w_ffn_up[l].astype(BF16),
                   w_ffn_down[l].astype(BF16), final_norm[None, :], l == DEPTH - 1, tm_tok)
    return x2d.reshape(batch, seq, d_model)
```

```python
import functools
import math

import jax
import jax.numpy as jnp
import numpy as np
from jax import lax
from jax.experimental import pallas as pl
from jax.experimental.pallas import tpu as pltpu

D_MODEL = 1024
SEQ = 2048
DEPTH = 2
A_HEADS = 4
A_HEAD_DIM = 64
B_HEADS = 4
B_HEAD_DIM = 128
B_PATTERNS = ((128, 1), (512, 4), (2048, 16))
C_HEADS = 8
C_NOPE_DIM = 64
C_ROPE_DIM = 32
C_V_DIM = 64
C_Q_LORA = 384
C_KV_LORA = 256
ROPE_THETA = 10000.0
FFN_HIDDEN = 2816
A_W = A_HEADS * 2 * A_HEAD_DIM
B_W = B_HEADS * B_HEAD_DIM
C_OUT_W = C_HEADS * C_V_DIM
C_Q_OFF = 2 * A_W + A_W + 3 * B_W
C_KV_OFF = C_Q_OFF + C_Q_LORA
C_PE_OFF = C_KV_OFF + C_KV_LORA
GATE_OFF = C_PE_OFF + C_ROPE_DIM
NORM_EPS = 1e-6
NEG_INF = -1e30
LOG2E = 1.4426950408889634

LANES = 128
T_ATT = 256
VMEM_LIMIT = 48 * 1024 * 1024

BF16 = jnp.bfloat16
F32 = jnp.float32


def _rms(xf, g):
    return xf * lax.rsqrt(jnp.mean(xf * xf, axis=-1, keepdims=True) + NORM_EPS) * g


def _nt_dot(a, b):
    return lax.dot_general(a, b, (((1,), (1,)), ((), ())), preferred_element_type=F32)


QKG_TN = 1024
QKG_W = 2 * A_W + 2 * B_W + 3 * D_MODEL
N_QK_BLOCKS = (2 * A_W + 2 * B_W) // QKG_TN


def _qkg_kernel(x_ref, g_ref, w_ref, cs_ref, o_ref, h_ref):
    j = pl.program_id(1)

    @pl.when(j == 0)
    def _():
        h_ref[...] = _rms(x_ref[...], g_ref[...]).astype(BF16)

    acc = jnp.dot(h_ref[...], w_ref[...], preferred_element_type=F32)

    @pl.when(j < N_QK_BLOCKS)
    def _():
        o_ref[...] = (acc * cs_ref[...]).astype(BF16)

    @pl.when(j >= N_QK_BLOCKS)
    def _():
        o_ref[...] = jax.nn.sigmoid(acc).astype(BF16)


def _qkg_proj(x2d, g, w, colscale, tm):
    m = x2d.shape[0]
    return pl.pallas_call(
        _qkg_kernel,
        grid=(m // tm, QKG_W // QKG_TN),
        in_specs=[
            pl.BlockSpec((tm, D_MODEL), lambda i, j: (i, 0)),
            pl.BlockSpec((1, D_MODEL), lambda i, j: (0, 0)),
            pl.BlockSpec((D_MODEL, QKG_TN), lambda i, j: (0, j)),
            pl.BlockSpec((1, QKG_TN), lambda i, j: (0, j)),
        ],
        out_specs=pl.BlockSpec((tm, QKG_TN), lambda i, j: (i, j)),
        out_shape=jax.ShapeDtypeStruct((m, QKG_W), BF16),
        scratch_shapes=[pltpu.VMEM((tm, D_MODEL), BF16)],
        compiler_params=pltpu.CompilerParams(
            dimension_semantics=("arbitrary", "arbitrary"), vmem_limit_bytes=VMEM_LIMIT),
        name="qkg_proj",
    )(x2d, g, w, colscale)


C_CW = 768
C_PE_GROUP = C_Q_LORA + C_KV_LORA


def _misc_kernel(x_ref, g_ref, wvt_ref, wc_ref, qn_ref, kvn_ref, wuq_ref, wuk_ref, wuvt_ref,
                 qtab_ref, ktab_ref, vta_ref, vtb_ref, qc_ref, kc_ref, vtc_ref):
    tm = x_ref.shape[0]
    h = _rms(x_ref[...], g_ref[...]).astype(BF16)

    vt = _nt_dot(wvt_ref[...], h).astype(BF16)
    vta_ref[...] = vt[:A_W]
    vtb_ref[...] = vt[A_W:]

    c = jnp.dot(h, wc_ref[...], preferred_element_type=F32)
    cqn = _rms(c[:, :C_Q_LORA], qn_ref[...]).astype(BF16)
    ckvn = _rms(c[:, C_Q_LORA:C_PE_GROUP], kvn_ref[...]).astype(BF16)

    q = jnp.dot(cqn, wuq_ref[...], preferred_element_type=F32)
    qtab = qtab_ref[...]
    for hd in range(C_HEADS):
        sl = slice(hd * LANES, (hd + 1) * LANES)
        qc_ref[:, sl] = (q[:, sl] * qtab).astype(BF16)

    kt = c[:, C_PE_GROUP:] * ktab_ref[...]
    lane = lax.broadcasted_iota(jnp.int32, kt.shape, 1)
    swapped = jnp.where(lane < 96, pltpu.roll(kt, 96, 1), pltpu.roll(kt, 32, 1))
    kp2 = jnp.where(lane >= 64, kt + swapped, 0.0)
    kk = jnp.dot(ckvn, wuk_ref[...], preferred_element_type=F32)
    for hd in range(C_HEADS):
        sl = slice(hd * LANES, (hd + 1) * LANES)
        kc_ref[:, sl] = (kk[:, sl] + kp2).astype(BF16)

    vtc_ref[...] = _nt_dot(wuvt_ref[...], ckvn).astype(BF16)


def _misc_proj(x2d, g, wvt, wc, qn, kvn, wuq, wuk, wuvt, qtab, ktab, batch, tm):
    m = x2d.shape[0]
    per_seq = SEQ // tm
    full = lambda shape: pl.BlockSpec(shape, lambda i: (0,) * len(shape))
    vt_spec = lambda w: pl.BlockSpec((None, w, tm), lambda i: (i // per_seq, 0, i % per_seq))
    return pl.pallas_call(
        _misc_kernel,
        grid=(m // tm,),
        in_specs=[
            pl.BlockSpec((tm, D_MODEL), lambda i: (i, 0)),
            full((1, D_MODEL)),
            full((A_W + B_W, D_MODEL)),
            full((D_MODEL, C_CW)),
            full((1, C_Q_LORA)),
            full((1, C_KV_LORA)),
            full((C_Q_LORA, C_HEADS * LANES)),
            full((C_KV_LORA, C_HEADS * LANES)),
            full((C_OUT_W, C_KV_LORA)),
            pl.BlockSpec((tm, LANES), lambda i: (i % per_seq, 0)),
            pl.BlockSpec((tm, LANES), lambda i: (i % per_seq, 0)),
        ],
        out_specs=[
            vt_spec(A_W), vt_spec(B_W),
            pl.BlockSpec((tm, C_HEADS * LANES), lambda i: (i, 0)),
            pl.BlockSpec((tm, C_HEADS * LANES), lambda i: (i, 0)),
            vt_spec(C_OUT_W),
        ],
        out_shape=[
            jax.ShapeDtypeStruct((batch, A_W, SEQ), BF16),
            jax.ShapeDtypeStruct((batch, B_W, SEQ), BF16),
            jax.ShapeDtypeStruct((m, C_HEADS * LANES), BF16),
            jax.ShapeDtypeStruct((m, C_HEADS * LANES), BF16),
            jax.ShapeDtypeStruct((batch, C_OUT_W, SEQ), BF16),
        ],
        compiler_params=pltpu.CompilerParams(
            dimension_semantics=("arbitrary",), vmem_limit_bytes=VMEM_LIMIT),
        name="misc_proj",
    )(x2d, g, wvt, wc, qn, kvn, wuq, wuk, wuvt, qtab, ktab)


NQ_ATT = SEQ // T_ATT


def _key_tile(jt):
    return slice(jt * T_ATT, (jt + 1) * T_ATT)


def _attend_tiles(qi, qs, k_tile, vt_keys, tab_ref, bias_off_diag, st_ref, p_ref):
    n_tiles = qi + 1
    r0 = tab_ref.shape[0] - n_tiles * T_ATT
    outs = []
    for s, q in enumerate(qs):
        m = None
        for jt in range(n_tiles):
            st = _nt_dot(k_tile(s, jt), q)
            if bias_off_diag or jt == qi:
                st = st + tab_ref[r0 + jt * T_ATT:r0 + (jt + 1) * T_ATT, :]
            st_ref[s, _key_tile(jt), :] = st
            mt = jnp.max(st, axis=0, keepdims=True)
            m = mt if m is None else jnp.maximum(m, mt)
        l = None
        for jt in range(n_tiles):
            p = jnp.exp2(st_ref[s, _key_tile(jt), :] - m)
            lt = jnp.sum(p, axis=0, keepdims=True)
            l = lt if l is None else l + lt
            p_ref[s, _key_tile(jt), :] = p.astype(BF16)
        n = n_tiles * T_ATT
        outs.append((jnp.dot(vt_keys(s, n), p_ref[s, :n, :], preferred_element_type=F32), l))
    return outs


def _attend(qs_of, k_tile, vt_keys, tab_ref, bias_off_diag, st_ref, p_ref, finish):
    for qi in range(NQ_ATT):
        slot = qi % st_ref.shape[0]
        outs = _attend_tiles(qi, qs_of(qi), k_tile, vt_keys, tab_ref, bias_off_diag,
                             st_ref.at[slot], p_ref.at[slot])
        finish(qi, outs)


def _diff_attn_kernel(lam_init, q_ref, k_ref, vt_ref, tab_ref, dl_ref, gain_ref, o_ref, st_ref, p_ref):
    dl = dl_ref[...]
    lam = (jnp.exp(jnp.sum(dl[0:1] * dl[1:2], axis=-1, keepdims=True))
           - jnp.exp(jnp.sum(dl[2:3] * dl[3:4], axis=-1, keepdims=True)) + lam_init)
    gain = gain_ref[...] * (1.0 - lam_init)
    lane = lax.broadcasted_iota(jnp.int32, (T_ATT, LANES), 1)

    def qs_of(qi):
        q = q_ref[_key_tile(qi), :]
        zero = jnp.zeros_like(q)
        return [jnp.where(lane < A_HEAD_DIM, q, zero), jnp.where(lane >= A_HEAD_DIM, q, zero)]

    def finish(qi, outs):
        (a0, l0), (a1, l1) = outs
        o = a0 / l0 - lam * (a1 / l1)
        y = o * lax.rsqrt(jnp.mean(o * o, axis=0, keepdims=True) + NORM_EPS) * gain
        o_ref[_key_tile(qi), :] = y.T.astype(BF16)

    _attend(qs_of, lambda s, jt: k_ref[_key_tile(jt), :], lambda s, n: vt_ref[:, :n],
            tab_ref, True, st_ref, p_ref, finish)


def _single_attn_kernel(q_ref, k_ref, vt_ref, tab_ref, o_ref, st_ref, p_ref):
    def finish(qi, outs):
        (a, l), = outs
        o_ref[_key_tile(qi), :] = (a / l).T.astype(BF16)

    _attend(lambda qi: [q_ref[_key_tile(qi), :]], lambda s, jt: k_ref[_key_tile(jt), :],
            lambda s, n: vt_ref[:, :n], tab_ref, True, st_ref, p_ref, finish)


def _pair_attn_kernel(q_ref, k_ref, vt_ref, tab_ref, o_ref, st_ref, p_ref):
    def finish(qi, outs):
        (a0, l0), (a1, l1) = outs
        o_ref[_key_tile(qi), :] = jnp.concatenate([a0 / l0, a1 / l1], axis=0).T.astype(BF16)

    _attend(lambda qi: [q_ref[_key_tile(qi), :LANES], q_ref[_key_tile(qi), LANES:]],
            lambda s, jt: k_ref[_key_tile(jt), s * LANES:(s + 1) * LANES],
            lambda s, n: vt_ref[s * C_V_DIM:(s + 1) * C_V_DIM, :n],
            tab_ref, False, st_ref, p_ref, finish)


ST_SLOTS = 2


def _flash_call(kernel, name, q_arr, q_col0, k_arr, k_col0, vt_arr, tab, extra, extra_specs,
                batch, heads, width, streams, tab_per_head):
    m = q_arr.shape[0]
    vrows = vt_arr.shape[1] // heads
    qb, kb = q_col0 // width, k_col0 // width
    in_specs = [
        pl.BlockSpec((SEQ, width), lambda h, b: (b, qb + h)),
        pl.BlockSpec((SEQ, width), lambda h, b: (b, kb + h)),
        pl.BlockSpec((None, vrows, SEQ), lambda h, b: (b, h, 0)),
        pl.BlockSpec((None,) + tab.shape[1:],
                     (lambda h, b: (h, 0, 0)) if tab_per_head else (lambda h, b: (0, 0, 0))),
    ] + extra_specs
    return pl.pallas_call(
        kernel,
        grid=(heads, batch),
        in_specs=in_specs,
        out_specs=pl.BlockSpec((SEQ, LANES), lambda h, b: (b, h)),
        out_shape=jax.ShapeDtypeStruct((m, heads * LANES), BF16),
        scratch_shapes=[
            pltpu.VMEM((ST_SLOTS, streams, SEQ, T_ATT), F32),
            pltpu.VMEM((ST_SLOTS, streams, SEQ, T_ATT), BF16),
        ],
        compiler_params=pltpu.CompilerParams(
            dimension_semantics=("arbitrary", "arbitrary"), vmem_limit_bytes=VMEM_LIMIT),
        name=name,
    )(q_arr, k_arr, vt_arr, tab, *extra)


def _mix_kernel(x_ref, oa_ref, ob_ref, oc_ref, ga_ref, gb_ref, gc_ref, wa_ref, wb_ref, wc_ref,
                wo_ref, o_ref):
    ya = jnp.dot(oa_ref[...], wa_ref[...], preferred_element_type=F32)
    yb = jnp.dot(ob_ref[...], wb_ref[...], preferred_element_type=F32)
    yc = jnp.dot(oc_ref[...], wc_ref[...], preferred_element_type=F32)
    mix = (ga_ref[...].astype(F32) * ya + gb_ref[...].astype(F32) * yb
           + gc_ref[...].astype(F32) * yc)
    o_ref[...] = x_ref[...] + jnp.dot(mix.astype(BF16), wo_ref[...], preferred_element_type=F32)


def _mix(x2d, oa, ob, oc, qkg, wa, wb, wc, wo, tm):
    m = x2d.shape[0]
    g0 = (2 * A_W + 2 * B_W) // D_MODEL
    row = lambda w: pl.BlockSpec((tm, w), lambda i: (i, 0))
    gate = lambda n: pl.BlockSpec((tm, D_MODEL), lambda i: (i, g0 + n))
    full = lambda shape: pl.BlockSpec(shape, lambda i: (0, 0))
    return pl.pallas_call(
        _mix_kernel,
        grid=(m // tm,),
        in_specs=[row(D_MODEL), row(A_W), row(B_W), row(C_OUT_W), gate(0), gate(1), gate(2),
                  full((A_W, D_MODEL)), full((B_W, D_MODEL)), full((C_OUT_W, D_MODEL)),
                  full((D_MODEL, D_MODEL))],
        out_specs=row(D_MODEL),
        out_shape=jax.ShapeDtypeStruct((m, D_MODEL), F32),
        compiler_params=pltpu.CompilerParams(
            dimension_semantics=("arbitrary",), vmem_limit_bytes=VMEM_LIMIT),
        name="mix",
    )(x2d, oa, ob, oc, qkg, qkg, qkg, wa, wb, wc, wo)


FFN_TH = FFN_HIDDEN // 2


def _ffn_kernel(final, x_ref, g_ref, wg_ref, wu_ref, wd_ref, fg_ref, o_ref, h_ref):
    j = pl.program_id(1)

    @pl.when(j == 0)
    def _():
        h_ref[...] = _rms(x_ref[...], g_ref[...]).astype(BF16)

    h = h_ref[...]
    gate = jnp.dot(h, wg_ref[...], preferred_element_type=F32)
    up = jnp.dot(h, wu_ref[...], preferred_element_type=F32)
    act = (gate * jax.nn.sigmoid(gate) * up).astype(BF16)
    part = jnp.dot(act, wd_ref[...], preferred_element_type=F32)

    @pl.when(j == 0)
    def _():
        o_ref[...] = x_ref[...] + part

    @pl.when(j > 0)
    def _():
        o_ref[...] += part

    if final:
        @pl.when(j == pl.num_programs(1) - 1)
        def _():
            o_ref[...] = _rms(o_ref[...], fg_ref[...])


def _ffn(x2d, g, wg, wu, wd, fg, final, tm):
    m = x2d.shape[0]
    return pl.pallas_call(
        functools.partial(_ffn_kernel, final),
        grid=(m // tm, FFN_HIDDEN // FFN_TH),
        in_specs=[
            pl.BlockSpec((tm, D_MODEL), lambda i, j: (i, 0)),
            pl.BlockSpec((1, D_MODEL), lambda i, j: (0, 0)),
            pl.BlockSpec((D_MODEL, FFN_TH), lambda i, j: (0, j)),
            pl.BlockSpec((D_MODEL, FFN_TH), lambda i, j: (0, j)),
            pl.BlockSpec((FFN_TH, D_MODEL), lambda i, j: (j, 0)),
            pl.BlockSpec((1, D_MODEL), lambda i, j: (0, 0)),
        ],
        out_specs=pl.BlockSpec((tm, D_MODEL), lambda i, j: (i, 0)),
        out_shape=jax.ShapeDtypeStruct((m, D_MODEL), F32),
        scratch_shapes=[pltpu.VMEM((tm, D_MODEL), BF16)],
        compiler_params=pltpu.CompilerParams(
            dimension_semantics=("arbitrary", "arbitrary"), vmem_limit_bytes=VMEM_LIMIT),
        name="ffn",
    )(x2d, g, wg, wu, wd, fg)


def _alibi_slopes():
    n = A_HEADS + B_HEADS
    s = 2.0 ** (-8.0 * np.arange(1, n + 1, dtype=np.float64) / n)
    return s[0::2], s[1::2]


def _bias_tables(slopes, multiplicity, rows):
    r = jnp.arange(rows, dtype=jnp.int32)[:, None]
    i = jnp.arange(T_ATT, dtype=jnp.int32)[None, :]
    d = (rows - T_ATT) - r + i
    mult = multiplicity(d)
    valid = (d >= 0) & (mult > 0)
    logm = jnp.log2(jnp.maximum(mult, 1).astype(F32))
    sl = jnp.asarray(np.asarray(slopes) * LOG2E, F32)[:, None, None]
    tab = logm[None] - sl * d.astype(F32)[None]
    return jnp.where(valid[None], tab, NEG_INF)


def _dilated_multiplicity(d):
    mult = jnp.zeros_like(d)
    for window, dil in B_PATTERNS:
        mult = mult + ((d % dil == 0) & (d // dil <= window // dil)).astype(jnp.int32)
    return mult


def _rope_tables():
    half = C_ROPE_DIM // 2
    inv_freq = ROPE_THETA ** (-jnp.arange(half, dtype=F32) / half)
    ang = jnp.arange(SEQ, dtype=F32)[:, None] * inv_freq[None, :]
    cos = jnp.concatenate([jnp.cos(ang)] * 2, axis=-1)
    sin = jnp.concatenate([jnp.sin(ang)] * 2, axis=-1)
    scale = (C_NOPE_DIM + C_ROPE_DIM) ** -0.5 * LOG2E
    qtab = jnp.concatenate([jnp.full((SEQ, C_NOPE_DIM), scale, F32), cos * scale, sin * scale], axis=-1)
    ktab = jnp.concatenate([jnp.zeros((SEQ, C_NOPE_DIM), F32), cos, sin], axis=-1)
    return qtab, ktab


def _rotate_half_cols(w):
    half = C_ROPE_DIM // 2
    return jnp.concatenate([-w[..., half:], w[..., :half]], axis=-1)


def _layer_weights(w_in, w_uq, w_ukv):
    sa, sb = A_HEAD_DIM ** -0.5 * LOG2E, B_HEAD_DIM ** -0.5 * LOG2E
    b0 = 3 * A_W
    w_qkg = jnp.concatenate(
        [w_in[:, :2 * A_W], w_in[:, b0:b0 + 2 * B_W], w_in[:, GATE_OFF:]], axis=1).astype(BF16)
    colscale = jnp.concatenate(
        [jnp.full((A_W,), sa, F32), jnp.ones((A_W,), F32), jnp.full((B_W,), sb, F32),
         jnp.ones((B_W,), F32), jnp.ones((3 * D_MODEL,), F32)])[None, :]
    w_vt = jnp.concatenate([w_in[:, 2 * A_W:3 * A_W], w_in[:, b0 + 2 * B_W:b0 + 3 * B_W]], axis=1).T.astype(BF16)
    w_pe = w_in[:, C_PE_OFF:GATE_OFF]
    w_c = jnp.concatenate(
        [w_in[:, C_Q_OFF:C_PE_OFF], jnp.zeros((D_MODEL, C_NOPE_DIM), F32), w_pe, _rotate_half_cols(w_pe)],
        axis=1).astype(BF16)
    uq = w_uq.reshape(C_Q_LORA, C_HEADS, C_NOPE_DIM + C_ROPE_DIM)
    uq_pe = uq[..., C_NOPE_DIM:]
    w_uq_ext = jnp.concatenate([uq, _rotate_half_cols(uq_pe)], axis=-1).reshape(C_Q_LORA, C_HEADS * LANES).astype(BF16)
    ukv = w_ukv.reshape(C_KV_LORA, C_HEADS, C_NOPE_DIM + C_V_DIM)
    w_uk = jnp.concatenate([ukv[..., :C_NOPE_DIM], jnp.zeros((C_KV_LORA, C_HEADS, LANES - C_NOPE_DIM), F32)],
                           axis=-1).reshape(C_KV_LORA, C_HEADS * LANES).astype(BF16)
    w_uvt = ukv[..., C_NOPE_DIM:].reshape(C_KV_LORA, C_OUT_W).T.astype(BF16)
    return w_qkg, colscale, w_vt, w_c, w_uq_ext, w_uk, w_uvt


def kernel(x, attn_norm, w_in, diff_lambda, diff_norm, mla_q_norm, mla_w_uq, mla_kv_norm, mla_w_ukv,
           w_branch_a, w_branch_b, w_branch_c, w_out, ffn_norm, w_ffn_gate, w_ffn_up, w_ffn_down,
           final_norm):
    batch, seq, d_model = x.shape
    assert (seq, d_model) == (SEQ, D_MODEL)
    m = batch * seq
    tm_proj = 1024
    tm_tok = 512

    slopes_a, slopes_b = _alibi_slopes()
    tab_a = _bias_tables(slopes_a, lambda d: jnp.ones_like(d), SEQ)
    tab_b = _bias_tables(slopes_b, _dilated_multiplicity, SEQ)
    tab_c = _bias_tables(np.zeros((1,)), lambda d: jnp.ones_like(d), T_ATT)
    qtab, ktab = _rope_tables()

    x2d = x.reshape(m, D_MODEL)
    for l in range(DEPTH):
        w_qkg, colscale, w_vt, w_c, w_uq_ext, w_uk, w_uvt = _layer_weights(w_in[l], mla_w_uq[l], mla_w_ukv[l])
        g_attn = attn_norm[l][None, :]
        qkg = _qkg_proj(x2d, g_attn, w_qkg, colscale, tm_proj)
        vta, vtb, qc, kc, vtc = _misc_proj(
            x2d, g_attn, w_vt, w_c, mla_q_norm[l][None, :], mla_kv_norm[l][None, :], w_uq_ext, w_uk, w_uvt,
            qtab, ktab, batch, tm_tok)

        lam_init = 0.8 - 0.6 * math.exp(-0.3 * l)
        oa = _flash_call(
            functools.partial(_diff_attn_kernel, lam_init), "diff_attn", qkg, 0, qkg, A_W, vta, tab_a,
            [diff_lambda[l], diff_norm[l][:, None]],
            [pl.BlockSpec((4, A_HEAD_DIM), lambda h, b: (0, 0)),
             pl.BlockSpec((2 * A_HEAD_DIM, 1), lambda h, b: (0, 0))],
            batch, A_HEADS, LANES, 2, True)
        ob = _flash_call(_single_attn_kernel, "dilated_attn", qkg, 2 * A_W, qkg, 2 * A_W + B_W, vtb, tab_b,
                         [], [], batch, B_HEADS, LANES, 1, True)
        oc = _flash_call(_pair_attn_kernel, "mla_attn", qc, 0, kc, 0, vtc, tab_c,
                         [], [], batch, C_HEADS // 2, 2 * LANES, 2, False)

        x2d = _mix(x2d, oa, ob, oc, qkg, w_branch_a[l].astype(BF16), w_branch_b[l].astype(BF16),
                   w_branch_c[l].astype(BF16), w_out[l].astype(BF16), tm_tok)
        x2d = _ffn(x2d, ffn_norm[l][None, :], w_ffn_gate[l].astype(BF16), w_ffn_up[l].astype(BF16),
                   w_ffn_down[l].astype(BF16), final_norm[None, :], l == DEPTH - 1, tm_tok)
    return x2d.reshape(batch, seq, d_model)
```

```python
import functools
import math

import jax
import jax.numpy as jnp
import numpy as np
from jax import lax
from jax.experimental import pallas as pl
from jax.experimental.pallas import tpu as pltpu

D_MODEL = 1024
SEQ = 2048
DEPTH = 2
A_HEADS = 4
A_HEAD_DIM = 64
B_HEADS = 4
B_HEAD_DIM = 128
B_PATTERNS = ((128, 1), (512, 4), (2048, 16))
C_HEADS = 8
C_NOPE_DIM = 64
C_ROPE_DIM = 32
C_V_DIM = 64
C_Q_LORA = 384
C_KV_LORA = 256
ROPE_THETA = 10000.0
FFN_HIDDEN = 2816
A_W = A_HEADS * 2 * A_HEAD_DIM
B_W = B_HEADS * B_HEAD_DIM
C_OUT_W = C_HEADS * C_V_DIM
C_Q_OFF = 2 * A_W + A_W + 3 * B_W
C_KV_OFF = C_Q_OFF + C_Q_LORA
C_PE_OFF = C_KV_OFF + C_KV_LORA
GATE_OFF = C_PE_OFF + C_ROPE_DIM
NORM_EPS = 1e-6
NEG_INF = -1e30
LOG2E = 1.4426950408889634

LANES = 128
T_ATT = 256
VMEM_LIMIT = 48 * 1024 * 1024

BF16 = jnp.bfloat16
F32 = jnp.float32


def _rms(xf, g):
    return xf * lax.rsqrt(jnp.mean(xf * xf, axis=-1, keepdims=True) + NORM_EPS) * g


def _nt_dot(a, b):
    return lax.dot_general(a, b, (((1,), (1,)), ((), ())), preferred_element_type=F32)


QK_W = 2 * A_W + 2 * B_W
C_CW = 768
C_PE_GROUP = C_Q_LORA + C_KV_LORA


def _resident(shape):
    return pl.BlockSpec(shape, lambda *_: (0,) * len(shape), pipeline_mode=pl.Buffered(1))


def _proj_kernel(x_ref, g_ref, wqk_ref, cs_ref, wvt_ref, wc_ref, qn_ref, kvn_ref, wuq_ref, wuk_ref,
                 wuvt_ref, qtab_ref, ktab_ref, qk_ref, vta_ref, vtb_ref, qc_ref, kc_ref, vtc_ref):
    h = _rms(x_ref[...], g_ref[...]).astype(BF16)

    qk_ref[...] = (jnp.dot(h, wqk_ref[...], preferred_element_type=F32) * cs_ref[...]).astype(BF16)

    vt = _nt_dot(wvt_ref[...], h).astype(BF16)
    vta_ref[...] = vt[:A_W]
    vtb_ref[...] = vt[A_W:]

    c = jnp.dot(h, wc_ref[...], preferred_element_type=F32)
    cqn = _rms(c[:, :C_Q_LORA], qn_ref[...]).astype(BF16)
    ckvn = _rms(c[:, C_Q_LORA:C_PE_GROUP], kvn_ref[...]).astype(BF16)

    q = jnp.dot(cqn, wuq_ref[...], preferred_element_type=F32)
    qtab = qtab_ref[...]
    for hd in range(C_HEADS):
        sl = slice(hd * LANES, (hd + 1) * LANES)
        qc_ref[:, sl] = (q[:, sl] * qtab).astype(BF16)

    kt = c[:, C_PE_GROUP:] * ktab_ref[...]
    lane = lax.broadcasted_iota(jnp.int32, kt.shape, 1)
    swapped = jnp.where(lane < 96, pltpu.roll(kt, 96, 1), pltpu.roll(kt, 32, 1))
    kp2 = jnp.where(lane >= 64, kt + swapped, 0.0)
    kk = jnp.dot(ckvn, wuk_ref[...], preferred_element_type=F32)
    for hd in range(C_HEADS):
        sl = slice(hd * LANES, (hd + 1) * LANES)
        kc_ref[:, sl] = (kk[:, sl] + kp2).astype(BF16)

    vtc_ref[...] = _nt_dot(wuvt_ref[...], ckvn).astype(BF16)


def _proj(x2d, g, wqk, colscale, wvt, wc, qn, kvn, wuq, wuk, wuvt, qtab, ktab, batch, tm):
    m = x2d.shape[0]
    per_seq = SEQ // tm
    row = lambda w: pl.BlockSpec((tm, w), lambda i: (i, 0))
    pos = pl.BlockSpec((tm, LANES), lambda i: (i % per_seq, 0))
    vt_spec = lambda w: pl.BlockSpec((None, w, tm), lambda i: (i // per_seq, 0, i % per_seq))
    return pl.pallas_call(
        _proj_kernel,
        grid=(m // tm,),
        in_specs=[
            row(D_MODEL),
            _resident((1, D_MODEL)),
            _resident((D_MODEL, QK_W)),
            _resident((1, QK_W)),
            _resident((A_W + B_W, D_MODEL)),
            _resident((D_MODEL, C_CW)),
            _resident((1, C_Q_LORA)),
            _resident((1, C_KV_LORA)),
            _resident((C_Q_LORA, C_HEADS * LANES)),
            _resident((C_KV_LORA, C_HEADS * LANES)),
            _resident((C_OUT_W, C_KV_LORA)),
            pos, pos,
        ],
        out_specs=[row(QK_W), vt_spec(A_W), vt_spec(B_W), row(C_HEADS * LANES), row(C_HEADS * LANES),
                   vt_spec(C_OUT_W)],
        out_shape=[
            jax.ShapeDtypeStruct((m, QK_W), BF16),
            jax.ShapeDtypeStruct((batch, A_W, SEQ), BF16),
            jax.ShapeDtypeStruct((batch, B_W, SEQ), BF16),
            jax.ShapeDtypeStruct((m, C_HEADS * LANES), BF16),
            jax.ShapeDtypeStruct((m, C_HEADS * LANES), BF16),
            jax.ShapeDtypeStruct((batch, C_OUT_W, SEQ), BF16),
        ],
        compiler_params=pltpu.CompilerParams(
            dimension_semantics=("arbitrary",), vmem_limit_bytes=VMEM_LIMIT),
        name="proj",
    )(x2d, g, wqk, colscale, wvt, wc, qn, kvn, wuq, wuk, wuvt, qtab, ktab)


NQ_ATT = SEQ // T_ATT


def _key_tile(jt):
    return slice(jt * T_ATT, (jt + 1) * T_ATT)


def _attend_tiles(qi, qs, k_tile, vt_keys, tab_ref, bias_off_diag, st_ref, p_ref):
    n_tiles = qi + 1
    r0 = tab_ref.shape[0] - n_tiles * T_ATT
    outs = []
    for s, q in enumerate(qs):
        m = None
        for jt in range(n_tiles):
            st = _nt_dot(k_tile(s, jt), q)
            if bias_off_diag or jt == qi:
                st = st + tab_ref[r0 + jt * T_ATT:r0 + (jt + 1) * T_ATT, :]
            st_ref[s, _key_tile(jt), :] = st
            mt = jnp.max(st, axis=0, keepdims=True)
            m = mt if m is None else jnp.maximum(m, mt)
        l = None
        for jt in range(n_tiles):
            p = jnp.exp2(st_ref[s, _key_tile(jt), :] - m)
            lt = jnp.sum(p, axis=0, keepdims=True)
            l = lt if l is None else l + lt
            p_ref[s, _key_tile(jt), :] = p.astype(BF16)
        n = n_tiles * T_ATT
        outs.append((jnp.dot(vt_keys(s, n), p_ref[s, :n, :], preferred_element_type=F32), l))
    return outs


def _attend(qs_of, k_tile, vt_keys, tab_ref, bias_off_diag, st_ref, p_ref, finish):
    for qi in range(NQ_ATT):
        slot = qi % st_ref.shape[0]
        outs = _attend_tiles(qi, qs_of(qi), k_tile, vt_keys, tab_ref, bias_off_diag,
                             st_ref.at[slot], p_ref.at[slot])
        finish(qi, outs)


def _diff_attn_kernel(lam_init, q_ref, k_ref, vt_ref, tab_ref, dl_ref, gain_ref, o_ref, st_ref, p_ref):
    dl = dl_ref[...]
    lam = (jnp.exp(jnp.sum(dl[0:1] * dl[1:2], axis=-1, keepdims=True))
           - jnp.exp(jnp.sum(dl[2:3] * dl[3:4], axis=-1, keepdims=True)) + lam_init)
    gain = gain_ref[...] * (1.0 - lam_init)
    lane = lax.broadcasted_iota(jnp.int32, (T_ATT, LANES), 1)

    def qs_of(qi):
        q = q_ref[_key_tile(qi), :]
        zero = jnp.zeros_like(q)
        return [jnp.where(lane < A_HEAD_DIM, q, zero), jnp.where(lane >= A_HEAD_DIM, q, zero)]

    def finish(qi, outs):
        (a0, l0), (a1, l1) = outs
        o = a0 / l0 - lam * (a1 / l1)
        y = o * lax.rsqrt(jnp.mean(o * o, axis=0, keepdims=True) + NORM_EPS) * gain
        o_ref[_key_tile(qi), :] = y.T.astype(BF16)

    _attend(qs_of, lambda s, jt: k_ref[_key_tile(jt), :], lambda s, n: vt_ref[:, :n],
            tab_ref, True, st_ref, p_ref, finish)


def _single_attn_kernel(q_ref, k_ref, vt_ref, tab_ref, o_ref, st_ref, p_ref):
    def finish(qi, outs):
        (a, l), = outs
        o_ref[_key_tile(qi), :] = (a / l).T.astype(BF16)

    _attend(lambda qi: [q_ref[_key_tile(qi), :]], lambda s, jt: k_ref[_key_tile(jt), :],
            lambda s, n: vt_ref[:, :n], tab_ref, True, st_ref, p_ref, finish)


def _pair_attn_kernel(q_ref, k_ref, vt_ref, tab_ref, o_ref, st_ref, p_ref):
    def finish(qi, outs):
        (a0, l0), (a1, l1) = outs
        o_ref[_key_tile(qi), :] = jnp.concatenate([a0 / l0, a1 / l1], axis=0).T.astype(BF16)

    _attend(lambda qi: [q_ref[_key_tile(qi), :LANES], q_ref[_key_tile(qi), LANES:]],
            lambda s, jt: k_ref[_key_tile(jt), s * LANES:(s + 1) * LANES],
            lambda s, n: vt_ref[s * C_V_DIM:(s + 1) * C_V_DIM, :n],
            tab_ref, False, st_ref, p_ref, finish)


ST_SLOTS = 2


def _flash_call(kernel, name, q_arr, q_col0, k_arr, k_col0, vt_arr, tab, extra, extra_specs,
                batch, heads, width, streams, tab_per_head):
    m = q_arr.shape[0]
    vrows = vt_arr.shape[1] // heads
    qb, kb = q_col0 // width, k_col0 // width
    in_specs = [
        pl.BlockSpec((SEQ, width), lambda h, b: (b, qb + h)),
        pl.BlockSpec((SEQ, width), lambda h, b: (b, kb + h)),
        pl.BlockSpec((None, vrows, SEQ), lambda h, b: (b, h, 0)),
        pl.BlockSpec((None,) + tab.shape[1:],
                     (lambda h, b: (h, 0, 0)) if tab_per_head else (lambda h, b: (0, 0, 0))),
    ] + extra_specs
    return pl.pallas_call(
        kernel,
        grid=(heads, batch),
        in_specs=in_specs,
        out_specs=pl.BlockSpec((SEQ, LANES), lambda h, b: (b, h)),
        out_shape=jax.ShapeDtypeStruct((m, heads * LANES), BF16),
        scratch_shapes=[
            pltpu.VMEM((ST_SLOTS, streams, SEQ, T_ATT), F32),
            pltpu.VMEM((ST_SLOTS, streams, SEQ, T_ATT), BF16),
        ],
        compiler_params=pltpu.CompilerParams(
            dimension_semantics=("arbitrary", "arbitrary"), vmem_limit_bytes=VMEM_LIMIT),
        name=name,
    )(q_arr, k_arr, vt_arr, tab, *extra)


def _mix_kernel(x_ref, g_ref, oa_ref, ob_ref, oc_ref, wg_ref, wa_ref, wb_ref, wc_ref, wo_ref, o_ref):
    x = x_ref[...]
    h = _rms(x, g_ref[...]).astype(BF16)
    mix = None
    for n, (o_br, w_br) in enumerate(((oa_ref, wa_ref), (ob_ref, wb_ref), (oc_ref, wc_ref))):
        gate = jax.nn.sigmoid(jnp.dot(h, wg_ref[:, n * D_MODEL:(n + 1) * D_MODEL], preferred_element_type=F32))
        term = gate * jnp.dot(o_br[...], w_br[...], preferred_element_type=F32)
        mix = term if mix is None else mix + term
    o_ref[...] = x + jnp.dot(mix.astype(BF16), wo_ref[...], preferred_element_type=F32)


def _mix(x2d, g, oa, ob, oc, wg, wa, wb, wc, wo, tm):
    m = x2d.shape[0]
    row = lambda w: pl.BlockSpec((tm, w), lambda i: (i, 0))
    return pl.pallas_call(
        _mix_kernel,
        grid=(m // tm,),
        in_specs=[row(D_MODEL), _resident((1, D_MODEL)), row(A_W), row(B_W), row(C_OUT_W),
                  _resident((D_MODEL, 3 * D_MODEL)), _resident((A_W, D_MODEL)), _resident((B_W, D_MODEL)),
                  _resident((C_OUT_W, D_MODEL)), _resident((D_MODEL, D_MODEL))],
        out_specs=row(D_MODEL),
        out_shape=jax.ShapeDtypeStruct((m, D_MODEL), F32),
        compiler_params=pltpu.CompilerParams(
            dimension_semantics=("arbitrary",), vmem_limit_bytes=VMEM_LIMIT),
        name="mix",
    )(x2d, g, oa, ob, oc, wg, wa, wb, wc, wo)


def _ffn_kernel(final, x_ref, g_ref, wg_ref, wu_ref, wd_ref, fg_ref, o_ref):
    x = x_ref[...]
    h = _rms(x, g_ref[...]).astype(BF16)
    gate = jnp.dot(h, wg_ref[...], preferred_element_type=F32)
    up = jnp.dot(h, wu_ref[...], preferred_element_type=F32)
    act = (gate * jax.nn.sigmoid(gate) * up).astype(BF16)
    y = x + jnp.dot(act, wd_ref[...], preferred_element_type=F32)
    o_ref[...] = _rms(y, fg_ref[...]) if final else y


def _ffn(x2d, g, wg, wu, wd, fg, final, tm):
    m = x2d.shape[0]
    return pl.pallas_call(
        functools.partial(_ffn_kernel, final),
        grid=(m // tm,),
        in_specs=[
            pl.BlockSpec((tm, D_MODEL), lambda i: (i, 0)),
            _resident((1, D_MODEL)),
            _resident((D_MODEL, FFN_HIDDEN)),
            _resident((D_MODEL, FFN_HIDDEN)),
            _resident((FFN_HIDDEN, D_MODEL)),
            _resident((1, D_MODEL)),
        ],
        out_specs=pl.BlockSpec((tm, D_MODEL), lambda i: (i, 0)),
        out_shape=jax.ShapeDtypeStruct((m, D_MODEL), F32),
        compiler_params=pltpu.CompilerParams(
            dimension_semantics=("arbitrary",), vmem_limit_bytes=VMEM_LIMIT),
        name="ffn",
    )(x2d, g, wg, wu, wd, fg)


def _alibi_slopes():
    n = A_HEADS + B_HEADS
    s = 2.0 ** (-8.0 * np.arange(1, n + 1, dtype=np.float64) / n)
    return s[0::2], s[1::2]


def _bias_tables(slopes, multiplicity, rows):
    r = jnp.arange(rows, dtype=jnp.int32)[:, None]
    i = jnp.arange(T_ATT, dtype=jnp.int32)[None, :]
    d = (rows - T_ATT) - r + i
    mult = multiplicity(d)
    valid = (d >= 0) & (mult > 0)
    logm = jnp.log2(jnp.maximum(mult, 1).astype(F32))
    sl = jnp.asarray(np.asarray(slopes) * LOG2E, F32)[:, None, None]
    tab = logm[None] - sl * d.astype(F32)[None]
    return jnp.where(valid[None], tab, NEG_INF)


def _dilated_multiplicity(d):
    mult = jnp.zeros_like(d)
    for window, dil in B_PATTERNS:
        mult = mult + ((d % dil == 0) & (d // dil <= window // dil)).astype(jnp.int32)
    return mult


def _rope_tables():
    half = C_ROPE_DIM // 2
    inv_freq = ROPE_THETA ** (-jnp.arange(half, dtype=F32) / half)
    ang = jnp.arange(SEQ, dtype=F32)[:, None] * inv_freq[None, :]
    cos = jnp.concatenate([jnp.cos(ang)] * 2, axis=-1)
    sin = jnp.concatenate([jnp.sin(ang)] * 2, axis=-1)
    scale = (C_NOPE_DIM + C_ROPE_DIM) ** -0.5 * LOG2E
    qtab = jnp.concatenate([jnp.full((SEQ, C_NOPE_DIM), scale, F32), cos * scale, sin * scale], axis=-1)
    ktab = jnp.concatenate([jnp.zeros((SEQ, C_NOPE_DIM), F32), cos, sin], axis=-1)
    return qtab, ktab


def _rotate_half_cols(w):
    half = C_ROPE_DIM // 2
    return jnp.concatenate([-w[..., half:], w[..., :half]], axis=-1)


def _layer_weights(w_in, w_uq, w_ukv):
    sa, sb = A_HEAD_DIM ** -0.5 * LOG2E, B_HEAD_DIM ** -0.5 * LOG2E
    b0 = 3 * A_W
    w_qk = jnp.concatenate([w_in[:, :2 * A_W], w_in[:, b0:b0 + 2 * B_W]], axis=1).astype(BF16)
    w_gate = w_in[:, GATE_OFF:].astype(BF16)
    colscale = jnp.concatenate(
        [jnp.full((A_W,), sa, F32), jnp.ones((A_W,), F32), jnp.full((B_W,), sb, F32),
         jnp.ones((B_W,), F32)])[None, :]
    w_vt = jnp.concatenate([w_in[:, 2 * A_W:3 * A_W], w_in[:, b0 + 2 * B_W:b0 + 3 * B_W]], axis=1).T.astype(BF16)
    w_pe = w_in[:, C_PE_OFF:GATE_OFF]
    w_c = jnp.concatenate(
        [w_in[:, C_Q_OFF:C_PE_OFF], jnp.zeros((D_MODEL, C_NOPE_DIM), F32), w_pe, _rotate_half_cols(w_pe)],
        axis=1).astype(BF16)
    uq = w_uq.reshape(C_Q_LORA, C_HEADS, C_NOPE_DIM + C_ROPE_DIM)
    uq_pe = uq[..., C_NOPE_DIM:]
    w_uq_ext = jnp.concatenate([uq, _rotate_half_cols(uq_pe)], axis=-1).reshape(C_Q_LORA, C_HEADS * LANES).astype(BF16)
    ukv = w_ukv.reshape(C_KV_LORA, C_HEADS, C_NOPE_DIM + C_V_DIM)
    w_uk = jnp.concatenate([ukv[..., :C_NOPE_DIM], jnp.zeros((C_KV_LORA, C_HEADS, LANES - C_NOPE_DIM), F32)],
                           axis=-1).reshape(C_KV_LORA, C_HEADS * LANES).astype(BF16)
    w_uvt = ukv[..., C_NOPE_DIM:].reshape(C_KV_LORA, C_OUT_W).T.astype(BF16)
    return w_qk, colscale, w_gate, w_vt, w_c, w_uq_ext, w_uk, w_uvt


def kernel(x, attn_norm, w_in, diff_lambda, diff_norm, mla_q_norm, mla_w_uq, mla_kv_norm, mla_w_ukv,
           w_branch_a, w_branch_b, w_branch_c, w_out, ffn_norm, w_ffn_gate, w_ffn_up, w_ffn_down,
           final_norm):
    batch, seq, d_model = x.shape
    assert (seq, d_model) == (SEQ, D_MODEL)
    m = batch * seq
    tm_tok = 512

    slopes_a, slopes_b = _alibi_slopes()
    tab_a = _bias_tables(slopes_a, lambda d: jnp.ones_like(d), SEQ)
    tab_b = _bias_tables(slopes_b, _dilated_multiplicity, SEQ)
    tab_c = _bias_tables(np.zeros((1,)), lambda d: jnp.ones_like(d), T_ATT)
    qtab, ktab = _rope_tables()

    x2d = x.reshape(m, D_MODEL)
    for l in range(DEPTH):
        w_qk, colscale, w_gate, w_vt, w_c, w_uq_ext, w_uk, w_uvt = _layer_weights(
            w_in[l], mla_w_uq[l], mla_w_ukv[l])
        g_attn = attn_norm[l][None, :]
        qk, vta, vtb, qc, kc, vtc = _proj(
            x2d, g_attn, w_qk, colscale, w_vt, w_c, mla_q_norm[l][None, :], mla_kv_norm[l][None, :],
            w_uq_ext, w_uk, w_uvt, qtab, ktab, batch, tm_tok)

        lam_init = 0.8 - 0.6 * math.exp(-0.3 * l)
        oa = _flash_call(
            functools.partial(_diff_attn_kernel, lam_init), "diff_attn", qk, 0, qk, A_W, vta, tab_a,
            [diff_lambda[l], diff_norm[l][:, None]],
            [pl.BlockSpec((4, A_HEAD_DIM), lambda h, b: (0, 0)),
             pl.BlockSpec((2 * A_HEAD_DIM, 1), lambda h, b: (0, 0))],
            batch, A_HEADS, LANES, 2, True)
        ob = _flash_call(_single_attn_kernel, "dilated_attn", qk, 2 * A_W, qk, 2 * A_W + B_W, vtb, tab_b,
                         [], [], batch, B_HEADS, LANES, 1, True)
        oc = _flash_call(_pair_attn_kernel, "mla_attn", qc, 0, kc, 0, vtc, tab_c,
                         [], [], batch, C_HEADS // 2, 2 * LANES, 2, False)

        x2d = _mix(x2d, g_attn, oa, ob, oc, w_gate, w_branch_a[l].astype(BF16), w_branch_b[l].astype(BF16),
                   w_branch_c[l].astype(BF16), w_out[l].astype(BF16), tm_tok)
        x2d = _ffn(x2d, ffn_norm[l][None, :], w_ffn_gate[l].astype(BF16), w_ffn_up[l].astype(BF16),
                   w_ffn_down[l].astype(BF16), final_norm[None, :], l == DEPTH - 1, tm_tok)
    return x2d.reshape(batch, seq, d_model)
```

```python
import functools
import math

import jax
import jax.numpy as jnp
import numpy as np
from jax import lax
from jax.experimental import pallas as pl
from jax.experimental.pallas import tpu as pltpu

D_MODEL = 1024
SEQ = 2048
DEPTH = 2
A_HEADS = 4
A_HEAD_DIM = 64
B_HEADS = 4
B_HEAD_DIM = 128
B_PATTERNS = ((128, 1), (512, 4), (2048, 16))
C_HEADS = 8
C_NOPE_DIM = 64
C_ROPE_DIM = 32
C_V_DIM = 64
C_Q_LORA = 384
C_KV_LORA = 256
ROPE_THETA = 10000.0
FFN_HIDDEN = 2816
A_W = A_HEADS * 2 * A_HEAD_DIM
B_W = B_HEADS * B_HEAD_DIM
C_OUT_W = C_HEADS * C_V_DIM
C_Q_OFF = 2 * A_W + A_W + 3 * B_W
C_KV_OFF = C_Q_OFF + C_Q_LORA
C_PE_OFF = C_KV_OFF + C_KV_LORA
GATE_OFF = C_PE_OFF + C_ROPE_DIM
NORM_EPS = 1e-6
NEG_INF = -1e30
LOG2E = 1.4426950408889634

LANES = 128
T_ATT = 256
VMEM_LIMIT = 48 * 1024 * 1024

BF16 = jnp.bfloat16
F32 = jnp.float32


def _rms(xf, g):
    return xf * lax.rsqrt(jnp.mean(xf * xf, axis=-1, keepdims=True) + NORM_EPS) * g


def _nt_dot(a, b):
    return lax.dot_general(a, b, (((1,), (1,)), ((), ())), preferred_element_type=F32)


QK_W = 2 * A_W + 2 * B_W
C_CW = 768
C_PE_GROUP = C_Q_LORA + C_KV_LORA


def _resident(shape):
    return pl.BlockSpec(shape, lambda *_: (0,) * len(shape), pipeline_mode=pl.Buffered(1))


def _proj_kernel(x_ref, g_ref, wqk_ref, cs_ref, wvt_ref, wc_ref, qn_ref, kvn_ref, wuq_ref, wuk_ref,
                 wuvt_ref, qtab_ref, ktab_ref, qk_ref, vta_ref, vtb_ref, qc_ref, kc_ref, vtc_ref):
    h = _rms(x_ref[...], g_ref[...]).astype(BF16)

    qk_ref[...] = (jnp.dot(h, wqk_ref[...], preferred_element_type=F32) * cs_ref[...]).astype(BF16)

    vt = _nt_dot(wvt_ref[...], h).astype(BF16)
    vta_ref[...] = vt[:A_W]
    vtb_ref[...] = vt[A_W:]

    c = jnp.dot(h, wc_ref[...], preferred_element_type=F32)
    cqn = _rms(c[:, :C_Q_LORA], qn_ref[...]).astype(BF16)
    ckvn = _rms(c[:, C_Q_LORA:C_PE_GROUP], kvn_ref[...]).astype(BF16)

    q = jnp.dot(cqn, wuq_ref[...], preferred_element_type=F32)
    qtab = qtab_ref[...]
    for hd in range(C_HEADS):
        sl = slice(hd * LANES, (hd + 1) * LANES)
        qc_ref[:, sl] = (q[:, sl] * qtab).astype(BF16)

    kt = c[:, C_PE_GROUP:] * ktab_ref[...]
    lane = lax.broadcasted_iota(jnp.int32, kt.shape, 1)
    swapped = jnp.where(lane < 96, pltpu.roll(kt, 96, 1), pltpu.roll(kt, 32, 1))
    kp2 = jnp.where(lane >= 64, kt + swapped, 0.0)
    kk = jnp.dot(ckvn, wuk_ref[...], preferred_element_type=F32)
    for hd in range(C_HEADS):
        sl = slice(hd * LANES, (hd + 1) * LANES)
        kc_ref[:, sl] = (kk[:, sl] + kp2).astype(BF16)

    vtc_ref[...] = _nt_dot(wuvt_ref[...], ckvn).astype(BF16)


def _proj(x2d, g, wqk, colscale, wvt, wc, qn, kvn, wuq, wuk, wuvt, qtab, ktab, batch, tm):
    m = x2d.shape[0]
    per_seq = SEQ // tm
    row = lambda w: pl.BlockSpec((tm, w), lambda i: (i, 0))
    pos = pl.BlockSpec((tm, LANES), lambda i: (i % per_seq, 0))
    vt_spec = lambda w: pl.BlockSpec((None, w, tm), lambda i: (i // per_seq, 0, i % per_seq))
    return pl.pallas_call(
        _proj_kernel,
        grid=(m // tm,),
        in_specs=[
            row(D_MODEL),
            _resident((1, D_MODEL)),
            _resident((D_MODEL, QK_W)),
            _resident((1, QK_W)),
            _resident((A_W + B_W, D_MODEL)),
            _resident((D_MODEL, C_CW)),
            _resident((1, C_Q_LORA)),
            _resident((1, C_KV_LORA)),
            _resident((C_Q_LORA, C_HEADS * LANES)),
            _resident((C_KV_LORA, C_HEADS * LANES)),
            _resident((C_OUT_W, C_KV_LORA)),
            pos, pos,
        ],
        out_specs=[row(QK_W), vt_spec(A_W), vt_spec(B_W), row(C_HEADS * LANES), row(C_HEADS * LANES),
                   vt_spec(C_OUT_W)],
        out_shape=[
            jax.ShapeDtypeStruct((m, QK_W), BF16),
            jax.ShapeDtypeStruct((batch, A_W, SEQ), BF16),
            jax.ShapeDtypeStruct((batch, B_W, SEQ), BF16),
            jax.ShapeDtypeStruct((m, C_HEADS * LANES), BF16),
            jax.ShapeDtypeStruct((m, C_HEADS * LANES), BF16),
            jax.ShapeDtypeStruct((batch, C_OUT_W, SEQ), BF16),
        ],
        compiler_params=pltpu.CompilerParams(
            dimension_semantics=("arbitrary",), vmem_limit_bytes=VMEM_LIMIT),
        name="proj",
    )(x2d, g, wqk, colscale, wvt, wc, qn, kvn, wuq, wuk, wuvt, qtab, ktab)


NQ_ATT = SEQ // T_ATT
SUM_ROWS = 16


def _key_tile(jt):
    return slice(jt * T_ATT, (jt + 1) * T_ATT)


def _attend(qs_of, k_tile, vt_tile, tab_ref, bias_off_diag, finish):
    ones_rows = (lax.broadcasted_iota(jnp.int32, (SUM_ROWS, T_ATT), 0) == 0).astype(BF16)
    prev = None
    for t in range(NQ_ATT + 1):
        if t < NQ_ATT:
            qs = qs_of(t)
            r0 = tab_ref.shape[0] - (t + 1) * T_ATT
            sts = [[] for _ in qs]
            ms = [None] * len(qs)
        if prev is not None:
            pst, pm = prev
            accs = [None] * len(pst)
        for jt in range(max(t + 1 if t < NQ_ATT else 0, t)):
            if t < NQ_ATT and jt <= t:
                for s, q in enumerate(qs):
                    st = _nt_dot(k_tile(s, jt), q)
                    if bias_off_diag or jt == t:
                        st = st + tab_ref[r0 + jt * T_ATT:r0 + (jt + 1) * T_ATT, :]
                    sts[s].append(st)
                    mt = jnp.max(st, axis=0, keepdims=True)
                    ms[s] = mt if ms[s] is None else jnp.maximum(ms[s], mt)
            if prev is not None and jt < t:
                for s in range(len(pst)):
                    p = jnp.exp2(pst[s][jt] - pm[s]).astype(BF16)
                    accs[s] = [p] if accs[s] is None else accs[s] + [p]
        if prev is not None:
            outs = []
            for s in range(len(pst)):
                vt = jnp.concatenate([jnp.concatenate([vt_tile(s, jt), ones_rows], axis=0) for jt in range(t)], axis=1)
                a = jnp.dot(vt, jnp.concatenate(accs[s], axis=0), preferred_element_type=F32)
                outs.append((a[:-SUM_ROWS], a[-SUM_ROWS:-SUM_ROWS + 1]))
            finish(t - 1, outs)
        prev = (sts, ms) if t < NQ_ATT else None


def _diff_attn_kernel(lam_init, q_ref, k_ref, vt_ref, tab_ref, dl_ref, gain_ref, o_ref):
    dl = dl_ref[...]
    lam = (jnp.exp(jnp.sum(dl[0:1] * dl[1:2], axis=-1, keepdims=True))
           - jnp.exp(jnp.sum(dl[2:3] * dl[3:4], axis=-1, keepdims=True)) + lam_init)
    gain = gain_ref[...] * (1.0 - lam_init)
    lane = lax.broadcasted_iota(jnp.int32, (T_ATT, LANES), 1)

    def qs_of(qi):
        q = q_ref[_key_tile(qi), :]
        zero = jnp.zeros_like(q)
        return [jnp.where(lane < A_HEAD_DIM, q, zero), jnp.where(lane >= A_HEAD_DIM, q, zero)]

    def finish(qi, outs):
        (a0, l0), (a1, l1) = outs
        o = a0 / l0 - lam * (a1 / l1)
        y = o * lax.rsqrt(jnp.mean(o * o, axis=0, keepdims=True) + NORM_EPS) * gain
        o_ref[_key_tile(qi), :] = y.T.astype(BF16)

    _attend(qs_of, lambda s, jt: k_ref[_key_tile(jt), :], lambda s, jt: vt_ref[:, _key_tile(jt)],
            tab_ref, True, finish)


def _single_attn_kernel(q_ref, k_ref, vt_ref, tab_ref, o_ref):
    def finish(qi, outs):
        (a, l), = outs
        o_ref[_key_tile(qi), :] = (a / l).T.astype(BF16)

    _attend(lambda qi: [q_ref[_key_tile(qi), :]], lambda s, jt: k_ref[_key_tile(jt), :],
            lambda s, jt: vt_ref[:, _key_tile(jt)], tab_ref, True, finish)


def _pair_attn_kernel(q_ref, k_ref, vt_ref, tab_ref, o_ref):
    def finish(qi, outs):
        (a0, l0), (a1, l1) = outs
        o_ref[_key_tile(qi), :] = jnp.concatenate([a0 / l0, a1 / l1], axis=0).T.astype(BF16)

    _attend(lambda qi: [q_ref[_key_tile(qi), :LANES], q_ref[_key_tile(qi), LANES:]],
            lambda s, jt: k_ref[_key_tile(jt), s * LANES:(s + 1) * LANES],
            lambda s, jt: vt_ref[s * C_V_DIM:(s + 1) * C_V_DIM, _key_tile(jt)],
            tab_ref, False, finish)


def _flash_call(kernel, name, q_arr, q_col0, k_arr, k_col0, vt_arr, tab, extra, extra_specs,
                batch, heads, width, tab_per_head):
    m = q_arr.shape[0]
    vrows = vt_arr.shape[1] // heads
    qb, kb = q_col0 // width, k_col0 // width
    in_specs = [
        pl.BlockSpec((SEQ, width), lambda h, b: (b, qb + h)),
        pl.BlockSpec((SEQ, width), lambda h, b: (b, kb + h)),
        pl.BlockSpec((None, vrows, SEQ), lambda h, b: (b, h, 0)),
        pl.BlockSpec((None,) + tab.shape[1:],
                     (lambda h, b: (h, 0, 0)) if tab_per_head else (lambda h, b: (0, 0, 0))),
    ] + extra_specs
    return pl.pallas_call(
        kernel,
        grid=(heads, batch),
        in_specs=in_specs,
        out_specs=pl.BlockSpec((SEQ, LANES), lambda h, b: (b, h)),
        out_shape=jax.ShapeDtypeStruct((m, heads * LANES), BF16),
        compiler_params=pltpu.CompilerParams(
            dimension_semantics=("arbitrary", "arbitrary"), vmem_limit_bytes=VMEM_LIMIT),
        name=name,
    )(q_arr, k_arr, vt_arr, tab, *extra)


def _mix_kernel(x_ref, g_ref, oa_ref, ob_ref, oc_ref, wg_ref, wa_ref, wb_ref, wc_ref, wo_ref, o_ref):
    x = x_ref[...]
    h = _rms(x, g_ref[...]).astype(BF16)
    mix = None
    for n, (o_br, w_br) in enumerate(((oa_ref, wa_ref), (ob_ref, wb_ref), (oc_ref, wc_ref))):
        gate = jax.nn.sigmoid(jnp.dot(h, wg_ref[:, n * D_MODEL:(n + 1) * D_MODEL], preferred_element_type=F32))
        term = gate * jnp.dot(o_br[...], w_br[...], preferred_element_type=F32)
        mix = term if mix is None else mix + term
    o_ref[...] = x + jnp.dot(mix.astype(BF16), wo_ref[...], preferred_element_type=F32)


def _mix(x2d, g, oa, ob, oc, wg, wa, wb, wc, wo, tm):
    m = x2d.shape[0]
    row = lambda w: pl.BlockSpec((tm, w), lambda i: (i, 0))
    return pl.pallas_call(
        _mix_kernel,
        grid=(m // tm,),
        in_specs=[row(D_MODEL), _resident((1, D_MODEL)), row(A_W), row(B_W), row(C_OUT_W),
                  _resident((D_MODEL, 3 * D_MODEL)), _resident((A_W, D_MODEL)), _resident((B_W, D_MODEL)),
                  _resident((C_OUT_W, D_MODEL)), _resident((D_MODEL, D_MODEL))],
        out_specs=row(D_MODEL),
        out_shape=jax.ShapeDtypeStruct((m, D_MODEL), F32),
        compiler_params=pltpu.CompilerParams(
            dimension_semantics=("arbitrary",), vmem_limit_bytes=VMEM_LIMIT),
        name="mix",
    )(x2d, g, oa, ob, oc, wg, wa, wb, wc, wo)


def _ffn_kernel(final, x_ref, g_ref, wg_ref, wu_ref, wd_ref, fg_ref, o_ref):
    x = x_ref[...]
    h = _rms(x, g_ref[...]).astype(BF16)
    gate = jnp.dot(h, wg_ref[...], preferred_element_type=F32)
    up = jnp.dot(h, wu_ref[...], preferred_element_type=F32)
    act = (gate * jax.nn.sigmoid(gate) * up).astype(BF16)
    y = x + jnp.dot(act, wd_ref[...], preferred_element_type=F32)
    o_ref[...] = _rms(y, fg_ref[...]) if final else y


def _ffn(x2d, g, wg, wu, wd, fg, final, tm):
    m = x2d.shape[0]
    return pl.pallas_call(
        functools.partial(_ffn_kernel, final),
        grid=(m // tm,),
        in_specs=[
            pl.BlockSpec((tm, D_MODEL), lambda i: (i, 0)),
            _resident((1, D_MODEL)),
            _resident((D_MODEL, FFN_HIDDEN)),
            _resident((D_MODEL, FFN_HIDDEN)),
            _resident((FFN_HIDDEN, D_MODEL)),
            _resident((1, D_MODEL)),
        ],
        out_specs=pl.BlockSpec((tm, D_MODEL), lambda i: (i, 0)),
        out_shape=jax.ShapeDtypeStruct((m, D_MODEL), F32),
        compiler_params=pltpu.CompilerParams(
            dimension_semantics=("arbitrary",), vmem_limit_bytes=VMEM_LIMIT),
        name="ffn",
    )(x2d, g, wg, wu, wd, fg)


def _alibi_slopes():
    n = A_HEADS + B_HEADS
    s = 2.0 ** (-8.0 * np.arange(1, n + 1, dtype=np.float64) / n)
    return s[0::2], s[1::2]


def _bias_tables(slopes, multiplicity, rows):
    r = jnp.arange(rows, dtype=jnp.int32)[:, None]
    i = jnp.arange(T_ATT, dtype=jnp.int32)[None, :]
    d = (rows - T_ATT) - r + i
    mult = multiplicity(d)
    valid = (d >= 0) & (mult > 0)
    logm = jnp.log2(jnp.maximum(mult, 1).astype(F32))
    sl = jnp.asarray(np.asarray(slopes) * LOG2E, F32)[:, None, None]
    tab = logm[None] - sl * d.astype(F32)[None]
    return jnp.where(valid[None], tab, NEG_INF)


def _dilated_multiplicity(d):
    mult = jnp.zeros_like(d)
    for window, dil in B_PATTERNS:
        mult = mult + ((d % dil == 0) & (d // dil <= window // dil)).astype(jnp.int32)
    return mult


def _rope_tables():
    half = C_ROPE_DIM // 2
    inv_freq = ROPE_THETA ** (-jnp.arange(half, dtype=F32) / half)
    ang = jnp.arange(SEQ, dtype=F32)[:, None] * inv_freq[None, :]
    cos = jnp.concatenate([jnp.cos(ang)] * 2, axis=-1)
    sin = jnp.concatenate([jnp.sin(ang)] * 2, axis=-1)
    scale = (C_NOPE_DIM + C_ROPE_DIM) ** -0.5 * LOG2E
    qtab = jnp.concatenate([jnp.full((SEQ, C_NOPE_DIM), scale, F32), cos * scale, sin * scale], axis=-1)
    ktab = jnp.concatenate([jnp.zeros((SEQ, C_NOPE_DIM), F32), cos, sin], axis=-1)
    return qtab, ktab


def _rotate_half_cols(w):
    half = C_ROPE_DIM // 2
    return jnp.concatenate([-w[..., half:], w[..., :half]], axis=-1)


def _layer_weights(w_in, w_uq, w_ukv):
    sa, sb = A_HEAD_DIM ** -0.5 * LOG2E, B_HEAD_DIM ** -0.5 * LOG2E
    b0 = 3 * A_W
    w_qk = jnp.concatenate([w_in[:, :2 * A_W], w_in[:, b0:b0 + 2 * B_W]], axis=1).astype(BF16)
    w_gate = w_in[:, GATE_OFF:].astype(BF16)
    colscale = jnp.concatenate(
        [jnp.full((A_W,), sa, F32), jnp.ones((A_W,), F32), jnp.full((B_W,), sb, F32),
         jnp.ones((B_W,), F32)])[None, :]
    w_vt = jnp.concatenate([w_in[:, 2 * A_W:3 * A_W], w_in[:, b0 + 2 * B_W:b0 + 3 * B_W]], axis=1).T.astype(BF16)
    w_pe = w_in[:, C_PE_OFF:GATE_OFF]
    w_c = jnp.concatenate(
        [w_in[:, C_Q_OFF:C_PE_OFF], jnp.zeros((D_MODEL, C_NOPE_DIM), F32), w_pe, _rotate_half_cols(w_pe)],
        axis=1).astype(BF16)
    uq = w_uq.reshape(C_Q_LORA, C_HEADS, C_NOPE_DIM + C_ROPE_DIM)
    uq_pe = uq[..., C_NOPE_DIM:]
    w_uq_ext = jnp.concatenate([uq, _rotate_half_cols(uq_pe)], axis=-1).reshape(C_Q_LORA, C_HEADS * LANES).astype(BF16)
    ukv = w_ukv.reshape(C_KV_LORA, C_HEADS, C_NOPE_DIM + C_V_DIM)
    w_uk = jnp.concatenate([ukv[..., :C_NOPE_DIM], jnp.zeros((C_KV_LORA, C_HEADS, LANES - C_NOPE_DIM), F32)],
                           axis=-1).reshape(C_KV_LORA, C_HEADS * LANES).astype(BF16)
    w_uvt = ukv[..., C_NOPE_DIM:].reshape(C_KV_LORA, C_OUT_W).T.astype(BF16)
    return w_qk, colscale, w_gate, w_vt, w_c, w_uq_ext, w_uk, w_uvt


def kernel(x, attn_norm, w_in, diff_lambda, diff_norm, mla_q_norm, mla_w_uq, mla_kv_norm, mla_w_ukv,
           w_branch_a, w_branch_b, w_branch_c, w_out, ffn_norm, w_ffn_gate, w_ffn_up, w_ffn_down,
           final_norm):
    batch, seq, d_model = x.shape
    assert (seq, d_model) == (SEQ, D_MODEL)
    m = batch * seq
    tm_tok = 512

    slopes_a, slopes_b = _alibi_slopes()
    tab_a = _bias_tables(slopes_a, lambda d: jnp.ones_like(d), SEQ)
    tab_b = _bias_tables(slopes_b, _dilated_multiplicity, SEQ)
    tab_c = _bias_tables(np.zeros((1,)), lambda d: jnp.ones_like(d), T_ATT)
    qtab, ktab = _rope_tables()

    x2d = x.reshape(m, D_MODEL)
    for l in range(DEPTH):
        w_qk, colscale, w_gate, w_vt, w_c, w_uq_ext, w_uk, w_uvt = _layer_weights(
            w_in[l], mla_w_uq[l], mla_w_ukv[l])
        g_attn = attn_norm[l][None, :]
        qk, vta, vtb, qc, kc, vtc = _proj(
            x2d, g_attn, w_qk, colscale, w_vt, w_c, mla_q_norm[l][None, :], mla_kv_norm[l][None, :],
            w_uq_ext, w_uk, w_uvt, qtab, ktab, batch, tm_tok)

        lam_init = 0.8 - 0.6 * math.exp(-0.3 * l)
        oa = _flash_call(
            functools.partial(_diff_attn_kernel, lam_init), "diff_attn", qk, 0, qk, A_W, vta, tab_a,
            [diff_lambda[l], diff_norm[l][:, None]],
            [pl.BlockSpec((4, A_HEAD_DIM), lambda h, b: (0, 0)),
             pl.BlockSpec((2 * A_HEAD_DIM, 1), lambda h, b: (0, 0))],
            batch, A_HEADS, LANES, True)
        ob = _flash_call(_single_attn_kernel, "dilated_attn", qk, 2 * A_W, qk, 2 * A_W + B_W, vtb, tab_b,
                         [], [], batch, B_HEADS, LANES, True)
        oc = _flash_call(_pair_attn_kernel, "mla_attn", qc, 0, kc, 0, vtc, tab_c,
                         [], [], batch, C_HEADS // 2, 2 * LANES, False)

        x2d = _mix(x2d, g_attn, oa, ob, oc, w_gate, w_branch_a[l].astype(BF16), w_branch_b[l].astype(BF16),
                   w_branch_c[l].astype(BF16), w_out[l].astype(BF16), tm_tok)
        x2d = _ffn(x2d, ffn_norm[l][None, :], w_ffn_gate[l].astype(BF16), w_ffn_up[l].astype(BF16),
                   w_ffn_down[l].astype(BF16), final_norm[None, :], l == DEPTH - 1, tm_tok)
    return x2d.reshape(batch, seq, d_model)
```

```python
import functools
import math

import jax
import jax.numpy as jnp
import numpy as np
from jax import lax
from jax.experimental import pallas as pl
from jax.experimental.pallas import tpu as pltpu

D_MODEL = 1024
SEQ = 2048
DEPTH = 2
A_HEADS = 4
A_HEAD_DIM = 64
B_HEADS = 4
B_HEAD_DIM = 128
B_PATTERNS = ((128, 1), (512, 4), (2048, 16))
C_HEADS = 8
C_NOPE_DIM = 64
C_ROPE_DIM = 32
C_V_DIM = 64
C_Q_LORA = 384
C_KV_LORA = 256
ROPE_THETA = 10000.0
FFN_HIDDEN = 2816
A_W = A_HEADS * 2 * A_HEAD_DIM
B_W = B_HEADS * B_HEAD_DIM
C_OUT_W = C_HEADS * C_V_DIM
C_Q_OFF = 2 * A_W + A_W + 3 * B_W
C_KV_OFF = C_Q_OFF + C_Q_LORA
C_PE_OFF = C_KV_OFF + C_KV_LORA
GATE_OFF = C_PE_OFF + C_ROPE_DIM
NORM_EPS = 1e-6
NEG_INF = -1e30
LOG2E = 1.4426950408889634

LANES = 128
T_ATT = 256
VMEM_LIMIT = 48 * 1024 * 1024

BF16 = jnp.bfloat16
F32 = jnp.float32


def _rms(xf, g):
    return xf * lax.rsqrt(jnp.mean(xf * xf, axis=-1, keepdims=True) + NORM_EPS) * g


def _nt_dot(a, b):
    return lax.dot_general(a, b, (((1,), (1,)), ((), ())), preferred_element_type=F32)


QK_W = 2 * A_W + 2 * B_W
C_CW = 768
C_PE_GROUP = C_Q_LORA + C_KV_LORA


def _resident(shape):
    return pl.BlockSpec(shape, lambda *_: (0,) * len(shape), pipeline_mode=pl.Buffered(1))


def _proj_kernel(x_ref, g_ref, wqk_ref, cs_ref, wvt_ref, wc_ref, qn_ref, kvn_ref, wuq_ref, wuk_ref,
                 wuvt_ref, qtab_ref, ktab_ref, qk_ref, vta_ref, vtb_ref, qc_ref, kc_ref, vtc_ref):
    h = _rms(x_ref[...], g_ref[...]).astype(BF16)

    qk_ref[...] = (jnp.dot(h, wqk_ref[...], preferred_element_type=F32) * cs_ref[...]).astype(BF16)

    vt = _nt_dot(wvt_ref[...], h).astype(BF16)
    vta_ref[...] = vt[:A_W]
    vtb_ref[...] = vt[A_W:]

    c = jnp.dot(h, wc_ref[...], preferred_element_type=F32)
    cqn = _rms(c[:, :C_Q_LORA], qn_ref[...]).astype(BF16)
    ckvn = _rms(c[:, C_Q_LORA:C_PE_GROUP], kvn_ref[...]).astype(BF16)

    q = jnp.dot(cqn, wuq_ref[...], preferred_element_type=F32)
    qtab = qtab_ref[...]
    for hd in range(C_HEADS):
        sl = slice(hd * LANES, (hd + 1) * LANES)
        qc_ref[:, sl] = (q[:, sl] * qtab).astype(BF16)

    kt = c[:, C_PE_GROUP:] * ktab_ref[...]
    lane = lax.broadcasted_iota(jnp.int32, kt.shape, 1)
    swapped = jnp.where(lane < 96, pltpu.roll(kt, 96, 1), pltpu.roll(kt, 32, 1))
    kp2 = jnp.where(lane >= 64, kt + swapped, 0.0)
    kk = jnp.dot(ckvn, wuk_ref[...], preferred_element_type=F32)
    for hd in range(C_HEADS):
        sl = slice(hd * LANES, (hd + 1) * LANES)
        kc_ref[:, sl] = (kk[:, sl] + kp2).astype(BF16)

    vtc_ref[...] = _nt_dot(wuvt_ref[...], ckvn).astype(BF16)


def _proj(x2d, g, wqk, colscale, wvt, wc, qn, kvn, wuq, wuk, wuvt, qtab, ktab, batch, tm):
    m = x2d.shape[0]
    per_seq = SEQ // tm
    row = lambda w: pl.BlockSpec((tm, w), lambda i: (i, 0))
    pos = pl.BlockSpec((tm, LANES), lambda i: (i % per_seq, 0))
    vt_spec = lambda w: pl.BlockSpec((None, w, tm), lambda i: (i // per_seq, 0, i % per_seq))
    return pl.pallas_call(
        _proj_kernel,
        grid=(m // tm,),
        in_specs=[
            row(D_MODEL),
            _resident((1, D_MODEL)),
            _resident((D_MODEL, QK_W)),
            _resident((1, QK_W)),
            _resident((A_W + B_W, D_MODEL)),
            _resident((D_MODEL, C_CW)),
            _resident((1, C_Q_LORA)),
            _resident((1, C_KV_LORA)),
            _resident((C_Q_LORA, C_HEADS * LANES)),
            _resident((C_KV_LORA, C_HEADS * LANES)),
            _resident((C_OUT_W, C_KV_LORA)),
            pos, pos,
        ],
        out_specs=[row(QK_W), vt_spec(A_W), vt_spec(B_W), row(C_HEADS * LANES), row(C_HEADS * LANES),
                   vt_spec(C_OUT_W)],
        out_shape=[
            jax.ShapeDtypeStruct((m, QK_W), BF16),
            jax.ShapeDtypeStruct((batch, A_W, SEQ), BF16),
            jax.ShapeDtypeStruct((batch, B_W, SEQ), BF16),
            jax.ShapeDtypeStruct((m, C_HEADS * LANES), BF16),
            jax.ShapeDtypeStruct((m, C_HEADS * LANES), BF16),
            jax.ShapeDtypeStruct((batch, C_OUT_W, SEQ), BF16),
        ],
        compiler_params=pltpu.CompilerParams(
            dimension_semantics=("arbitrary",), vmem_limit_bytes=VMEM_LIMIT),
        name="proj",
    )(x2d, g, wqk, colscale, wvt, wc, qn, kvn, wuq, wuk, wuvt, qtab, ktab)


NQ_ATT = SEQ // T_ATT
KPOS_TERMS = 3
SUM_ROWS = 16


def _key_tile(jt):
    return slice(jt * T_ATT, (jt + 1) * T_ATT)


def _attend(qs_of, k_rows, vt_cols, tab_ref, bias_off_diag, finish):
    scored = None
    probs = None
    for t in range(NQ_ATT + 2):
        new_scored = None
        if t < NQ_ATT:
            n = (t + 1) * T_ATT
            new_scored = []
            for s, q in enumerate(qs_of(t)):
                st = _nt_dot(k_rows(s, n), q)
                if bias_off_diag:
                    st = st + tab_ref[tab_ref.shape[0] - n:, :]
                elif t == 0:
                    st = st + tab_ref[...]
                else:
                    st = jnp.concatenate([st[:n - T_ATT], st[n - T_ATT:] + tab_ref[...]], axis=0)
                new_scored.append((st, jnp.max(st, axis=0, keepdims=True)))
        new_probs = None
        if scored is not None:
            new_probs = [jnp.exp2(st - m).astype(BF16) for st, m in scored]
        if probs is not None:
            n = (t - 1) * T_ATT
            ones_rows = (lax.broadcasted_iota(jnp.int32, (SUM_ROWS, n), 0) == 0).astype(BF16)
            outs = []
            for s, p in enumerate(probs):
                vt = jnp.concatenate([vt_cols(s, n), ones_rows], axis=0)
                a = jnp.dot(vt, p, preferred_element_type=F32)
                outs.append((a[:-SUM_ROWS], a[-SUM_ROWS:-SUM_ROWS + 1]))
            finish(t - 2, outs)
        scored, probs = new_scored, new_probs


def _diff_attn_kernel(lam_init, q_ref, k_ref, vt_ref, tab_ref, kpos_ref, dl_ref, gain_ref, o_ref):
    dl = dl_ref[...]
    lam = (jnp.exp(jnp.sum(dl[0:1] * dl[1:2], axis=-1, keepdims=True))
           - jnp.exp(jnp.sum(dl[2:3] * dl[3:4], axis=-1, keepdims=True)) + lam_init)
    gain = gain_ref[...] * (1.0 - lam_init)
    lane = lax.broadcasted_iota(jnp.int32, (T_ATT, LANES), 1)
    q_ones = (lane < KPOS_TERMS).astype(BF16)

    def qs_of(qi):
        q = q_ref[_key_tile(qi), :]
        zero = jnp.zeros_like(q)
        return [jnp.concatenate([jnp.where(lane < A_HEAD_DIM, q, zero), q_ones], axis=1),
                jnp.concatenate([jnp.where(lane >= A_HEAD_DIM, q, zero), q_ones], axis=1)]

    def finish(qi, outs):
        (a0, l0), (a1, l1) = outs
        o = a0 / l0 - lam * (a1 / l1)
        y = o * lax.rsqrt(jnp.mean(o * o, axis=0, keepdims=True) + NORM_EPS) * gain
        o_ref[_key_tile(qi), :] = y.T.astype(BF16)

    _attend(qs_of, lambda s, n: jnp.concatenate([k_ref[:n, :], kpos_ref[:n, :]], axis=1),
            lambda s, n: vt_ref[:, :n], tab_ref, False, finish)


def _single_attn_kernel(q_ref, k_ref, vt_ref, tab_ref, o_ref):
    def finish(qi, outs):
        (a, l), = outs
        o_ref[_key_tile(qi), :] = (a / l).T.astype(BF16)

    _attend(lambda qi: [q_ref[_key_tile(qi), :]], lambda s, n: k_ref[:n, :],
            lambda s, n: vt_ref[:, :n], tab_ref, True, finish)


def _pair_attn_kernel(q_ref, k_ref, vt_ref, tab_ref, o_ref):
    def finish(qi, outs):
        (a0, l0), (a1, l1) = outs
        o_ref[_key_tile(qi), :] = jnp.concatenate([a0 / l0, a1 / l1], axis=0).T.astype(BF16)

    _attend(lambda qi: [q_ref[_key_tile(qi), :LANES], q_ref[_key_tile(qi), LANES:]],
            lambda s, n: k_ref[:n, s * LANES:(s + 1) * LANES],
            lambda s, n: vt_ref[s * C_V_DIM:(s + 1) * C_V_DIM, :n],
            tab_ref, False, finish)


def _flash_call(kernel, name, q_arr, q_col0, k_arr, k_col0, vt_arr, tab, extra, extra_specs,
                batch, heads, width, tab_per_head):
    m = q_arr.shape[0]
    vrows = vt_arr.shape[1] // heads
    qb, kb = q_col0 // width, k_col0 // width
    in_specs = [
        pl.BlockSpec((SEQ, width), lambda h, b: (b, qb + h)),
        pl.BlockSpec((SEQ, width), lambda h, b: (b, kb + h)),
        pl.BlockSpec((None, vrows, SEQ), lambda h, b: (b, h, 0)),
        pl.BlockSpec((None,) + tab.shape[1:],
                     (lambda h, b: (h, 0, 0)) if tab_per_head else (lambda h, b: (0, 0, 0))),
    ] + extra_specs
    return pl.pallas_call(
        kernel,
        grid=(heads, batch),
        in_specs=in_specs,
        out_specs=pl.BlockSpec((SEQ, LANES), lambda h, b: (b, h)),
        out_shape=jax.ShapeDtypeStruct((m, heads * LANES), BF16),
        compiler_params=pltpu.CompilerParams(
            dimension_semantics=("arbitrary", "arbitrary"), vmem_limit_bytes=VMEM_LIMIT),
        name=name,
    )(q_arr, k_arr, vt_arr, tab, *extra)


def _mix_kernel(x_ref, g_ref, oa_ref, ob_ref, oc_ref, wg_ref, wa_ref, wb_ref, wc_ref, wo_ref, o_ref):
    x = x_ref[...]
    h = _rms(x, g_ref[...]).astype(BF16)
    mix = None
    for n, (o_br, w_br) in enumerate(((oa_ref, wa_ref), (ob_ref, wb_ref), (oc_ref, wc_ref))):
        gate = jax.nn.sigmoid(jnp.dot(h, wg_ref[:, n * D_MODEL:(n + 1) * D_MODEL], preferred_element_type=F32))
        term = gate * jnp.dot(o_br[...], w_br[...], preferred_element_type=F32)
        mix = term if mix is None else mix + term
    o_ref[...] = x + jnp.dot(mix.astype(BF16), wo_ref[...], preferred_element_type=F32)


def _mix(x2d, g, oa, ob, oc, wg, wa, wb, wc, wo, tm):
    m = x2d.shape[0]
    row = lambda w: pl.BlockSpec((tm, w), lambda i: (i, 0))
    return pl.pallas_call(
        _mix_kernel,
        grid=(m // tm,),
        in_specs=[row(D_MODEL), _resident((1, D_MODEL)), row(A_W), row(B_W), row(C_OUT_W),
                  _resident((D_MODEL, 3 * D_MODEL)), _resident((A_W, D_MODEL)), _resident((B_W, D_MODEL)),
                  _resident((C_OUT_W, D_MODEL)), _resident((D_MODEL, D_MODEL))],
        out_specs=row(D_MODEL),
        out_shape=jax.ShapeDtypeStruct((m, D_MODEL), F32),
        compiler_params=pltpu.CompilerParams(
            dimension_semantics=("arbitrary",), vmem_limit_bytes=VMEM_LIMIT),
        name="mix",
    )(x2d, g, oa, ob, oc, wg, wa, wb, wc, wo)


def _ffn_kernel(final, x_ref, g_ref, wg_ref, wu_ref, wd_ref, fg_ref, o_ref):
    x = x_ref[...]
    h = _rms(x, g_ref[...]).astype(BF16)
    gate = jnp.dot(h, wg_ref[...], preferred_element_type=F32)
    up = jnp.dot(h, wu_ref[...], preferred_element_type=F32)
    act = (gate * jax.nn.sigmoid(gate) * up).astype(BF16)
    y = x + jnp.dot(act, wd_ref[...], preferred_element_type=F32)
    o_ref[...] = _rms(y, fg_ref[...]) if final else y


def _ffn(x2d, g, wg, wu, wd, fg, final, tm):
    m = x2d.shape[0]
    return pl.pallas_call(
        functools.partial(_ffn_kernel, final),
        grid=(m // tm,),
        in_specs=[
            pl.BlockSpec((tm, D_MODEL), lambda i: (i, 0)),
            _resident((1, D_MODEL)),
            _resident((D_MODEL, FFN_HIDDEN)),
            _resident((D_MODEL, FFN_HIDDEN)),
            _resident((FFN_HIDDEN, D_MODEL)),
            _resident((1, D_MODEL)),
        ],
        out_specs=pl.BlockSpec((tm, D_MODEL), lambda i: (i, 0)),
        out_shape=jax.ShapeDtypeStruct((m, D_MODEL), F32),
        compiler_params=pltpu.CompilerParams(
            dimension_semantics=("arbitrary",), vmem_limit_bytes=VMEM_LIMIT),
        name="ffn",
    )(x2d, g, wg, wu, wd, fg)


def _alibi_slopes():
    n = A_HEADS + B_HEADS
    s = 2.0 ** (-8.0 * np.arange(1, n + 1, dtype=np.float64) / n)
    return s[0::2], s[1::2]


def _bias_tables(slopes, multiplicity, rows):
    r = jnp.arange(rows, dtype=jnp.int32)[:, None]
    i = jnp.arange(T_ATT, dtype=jnp.int32)[None, :]
    d = (rows - T_ATT) - r + i
    mult = multiplicity(d)
    valid = (d >= 0) & (mult > 0)
    logm = jnp.log2(jnp.maximum(mult, 1).astype(F32))
    sl = jnp.asarray(np.asarray(slopes) * LOG2E, F32)[:, None, None]
    tab = logm[None] - sl * d.astype(F32)[None]
    return jnp.where(valid[None], tab, NEG_INF)


def _kpos_tables(slopes):
    v = (np.asarray(slopes) * LOG2E).astype(np.float32)[:, None] * np.arange(SEQ, dtype=np.float32)[None, :]
    tab = np.zeros(v.shape + (LANES,), BF16)
    for term in range(KPOS_TERMS):
        piece = v.astype(BF16)
        tab[..., term] = piece
        v = v - piece.astype(np.float32)
    return jnp.asarray(tab)


def _dilated_multiplicity(d):
    mult = jnp.zeros_like(d)
    for window, dil in B_PATTERNS:
        mult = mult + ((d % dil == 0) & (d // dil <= window // dil)).astype(jnp.int32)
    return mult


def _rope_tables():
    half = C_ROPE_DIM // 2
    inv_freq = ROPE_THETA ** (-jnp.arange(half, dtype=F32) / half)
    ang = jnp.arange(SEQ, dtype=F32)[:, None] * inv_freq[None, :]
    cos = jnp.concatenate([jnp.cos(ang)] * 2, axis=-1)
    sin = jnp.concatenate([jnp.sin(ang)] * 2, axis=-1)
    scale = (C_NOPE_DIM + C_ROPE_DIM) ** -0.5 * LOG2E
    qtab = jnp.concatenate([jnp.full((SEQ, C_NOPE_DIM), scale, F32), cos * scale, sin * scale], axis=-1)
    ktab = jnp.concatenate([jnp.zeros((SEQ, C_NOPE_DIM), F32), cos, sin], axis=-1)
    return qtab, ktab


def _rotate_half_cols(w):
    half = C_ROPE_DIM // 2
    return jnp.concatenate([-w[..., half:], w[..., :half]], axis=-1)


def _layer_weights(w_in, w_uq, w_ukv):
    sa, sb = A_HEAD_DIM ** -0.5 * LOG2E, B_HEAD_DIM ** -0.5 * LOG2E
    b0 = 3 * A_W
    w_qk = jnp.concatenate([w_in[:, :2 * A_W], w_in[:, b0:b0 + 2 * B_W]], axis=1).astype(BF16)
    w_gate = w_in[:, GATE_OFF:].astype(BF16)
    colscale = jnp.concatenate(
        [jnp.full((A_W,), sa, F32), jnp.ones((A_W,), F32), jnp.full((B_W,), sb, F32),
         jnp.ones((B_W,), F32)])[None, :]
    w_vt = jnp.concatenate([w_in[:, 2 * A_W:3 * A_W], w_in[:, b0 + 2 * B_W:b0 + 3 * B_W]], axis=1).T.astype(BF16)
    w_pe = w_in[:, C_PE_OFF:GATE_OFF]
    w_c = jnp.concatenate(
        [w_in[:, C_Q_OFF:C_PE_OFF], jnp.zeros((D_MODEL, C_NOPE_DIM), F32), w_pe, _rotate_half_cols(w_pe)],
        axis=1).astype(BF16)
    uq = w_uq.reshape(C_Q_LORA, C_HEADS, C_NOPE_DIM + C_ROPE_DIM)
    uq_pe = uq[..., C_NOPE_DIM:]
    w_uq_ext = jnp.concatenate([uq, _rotate_half_cols(uq_pe)], axis=-1).reshape(C_Q_LORA, C_HEADS * LANES).astype(BF16)
    ukv = w_ukv.reshape(C_KV_LORA, C_HEADS, C_NOPE_DIM + C_V_DIM)
    w_uk = jnp.concatenate([ukv[..., :C_NOPE_DIM], jnp.zeros((C_KV_LORA, C_HEADS, LANES - C_NOPE_DIM), F32)],
                           axis=-1).reshape(C_KV_LORA, C_HEADS * LANES).astype(BF16)
    w_uvt = ukv[..., C_NOPE_DIM:].reshape(C_KV_LORA, C_OUT_W).T.astype(BF16)
    return w_qk, colscale, w_gate, w_vt, w_c, w_uq_ext, w_uk, w_uvt


def kernel(x, attn_norm, w_in, diff_lambda, diff_norm, mla_q_norm, mla_w_uq, mla_kv_norm, mla_w_ukv,
           w_branch_a, w_branch_b, w_branch_c, w_out, ffn_norm, w_ffn_gate, w_ffn_up, w_ffn_down,
           final_norm):
    batch, seq, d_model = x.shape
    assert (seq, d_model) == (SEQ, D_MODEL)
    m = batch * seq
    tm_tok = 512

    slopes_a, slopes_b = _alibi_slopes()
    kpos_a = _kpos_tables(slopes_a)
    tab_b = _bias_tables(slopes_b, _dilated_multiplicity, SEQ)
    tab_c = _bias_tables(np.zeros((1,)), lambda d: jnp.ones_like(d), T_ATT)
    qtab, ktab = _rope_tables()

    x2d = x.reshape(m, D_MODEL)
    for l in range(DEPTH):
        w_qk, colscale, w_gate, w_vt, w_c, w_uq_ext, w_uk, w_uvt = _layer_weights(
            w_in[l], mla_w_uq[l], mla_w_ukv[l])
        g_attn = attn_norm[l][None, :]
        qk, vta, vtb, qc, kc, vtc = _proj(
            x2d, g_attn, w_qk, colscale, w_vt, w_c, mla_q_norm[l][None, :], mla_kv_norm[l][None, :],
            w_uq_ext, w_uk, w_uvt, qtab, ktab, batch, tm_tok)

        lam_init = 0.8 - 0.6 * math.exp(-0.3 * l)
        oa = _flash_call(
            functools.partial(_diff_attn_kernel, lam_init), "diff_attn", qk, 0, qk, A_W, vta, tab_c,
            [kpos_a, diff_lambda[l], diff_norm[l][:, None]],
            [pl.BlockSpec((None, SEQ, LANES), lambda h, b: (h, 0, 0)),
             pl.BlockSpec((4, A_HEAD_DIM), lambda h, b: (0, 0)),
             pl.BlockSpec((2 * A_HEAD_DIM, 1), lambda h, b: (0, 0))],
            batch, A_HEADS, LANES, False)
        ob = _flash_call(_single_attn_kernel, "dilated_attn", qk, 2 * A_W, qk, 2 * A_W + B_W, vtb, tab_b,
                         [], [], batch, B_HEADS, LANES, True)
        oc = _flash_call(_pair_attn_kernel, "mla_attn", qc, 0, kc, 0, vtc, tab_c,
                         [], [], batch, C_HEADS // 2, 2 * LANES, False)

        x2d = _mix(x2d, g_attn, oa, ob, oc, w_gate, w_branch_a[l].astype(BF16), w_branch_b[l].astype(BF16),
                   w_branch_c[l].astype(BF16), w_out[l].astype(BF16), tm_tok)
        x2d = _ffn(x2d, ffn_norm[l][None, :], w_ffn_gate[l].astype(BF16), w_ffn_up[l].astype(BF16),
                   w_ffn_down[l].astype(BF16), final_norm[None, :], l == DEPTH - 1, tm_tok)
    return x2d.reshape(batch, seq, d_model)
```

```python
import functools
import math

import jax
import jax.numpy as jnp
import numpy as np
from jax import lax
from jax.experimental import pallas as pl
from jax.experimental.pallas import tpu as pltpu

D_MODEL = 1024
SEQ = 2048
DEPTH = 2
A_HEADS = 4
A_HEAD_DIM = 64
B_HEADS = 4
B_HEAD_DIM = 128
B_PATTERNS = ((128, 1), (512, 4), (2048, 16))
C_HEADS = 8
C_NOPE_DIM = 64
C_ROPE_DIM = 32
C_V_DIM = 64
C_Q_LORA = 384
C_KV_LORA = 256
ROPE_THETA = 10000.0
FFN_HIDDEN = 2816
A_W = A_HEADS * 2 * A_HEAD_DIM
B_W = B_HEADS * B_HEAD_DIM
C_OUT_W = C_HEADS * C_V_DIM
C_Q_OFF = 2 * A_W + A_W + 3 * B_W
C_KV_OFF = C_Q_OFF + C_Q_LORA
C_PE_OFF = C_KV_OFF + C_KV_LORA
GATE_OFF = C_PE_OFF + C_ROPE_DIM
NORM_EPS = 1e-6
NEG_INF = -1e30
LOG2E = 1.4426950408889634

LANES = 128
T_ATT = 256
VMEM_LIMIT = 48 * 1024 * 1024

BF16 = jnp.bfloat16
F32 = jnp.float32


def _rms(xf, g):
    return xf * lax.rsqrt(jnp.mean(xf * xf, axis=-1, keepdims=True) + NORM_EPS) * g


def _nt_dot(a, b):
    return lax.dot_general(a, b, (((1,), (1,)), ((), ())), preferred_element_type=F32)


QK_W = 2 * A_W + 2 * B_W
C_CW = 768
C_PE_GROUP = C_Q_LORA + C_KV_LORA


def _resident(shape):
    return pl.BlockSpec(shape, lambda *_: (0,) * len(shape), pipeline_mode=pl.Buffered(1))


def _proj_kernel(x_ref, g_ref, wqk_ref, cs_ref, wvt_ref, wc_ref, qn_ref, kvn_ref, wuq_ref, wuk_ref,
                 wuvt_ref, qtab_ref, ktab_ref, qk_ref, vta_ref, vtb_ref, qc_ref, kc_ref, vtc_ref):
    h = _rms(x_ref[...], g_ref[...]).astype(BF16)

    qk = (jnp.dot(h, wqk_ref[...], preferred_element_type=F32) * cs_ref[...]).astype(BF16)
    for hd in range(QK_W // LANES):
        qk_ref[hd] = qk[:, hd * LANES:(hd + 1) * LANES]

    vt = _nt_dot(wvt_ref[...], h).astype(BF16)
    vta_ref[...] = vt[:A_W]
    vtb_ref[...] = vt[A_W:]

    c = jnp.dot(h, wc_ref[...], preferred_element_type=F32)
    cqn = _rms(c[:, :C_Q_LORA], qn_ref[...]).astype(BF16)
    ckvn = _rms(c[:, C_Q_LORA:C_PE_GROUP], kvn_ref[...]).astype(BF16)

    q = jnp.dot(cqn, wuq_ref[...], preferred_element_type=F32)
    qtab = qtab_ref[...]
    for hd in range(C_HEADS):
        sl = slice(hd * LANES, (hd + 1) * LANES)
        qc_ref[hd] = (q[:, sl] * qtab).astype(BF16)

    kt = c[:, C_PE_GROUP:] * ktab_ref[...]
    lane = lax.broadcasted_iota(jnp.int32, kt.shape, 1)
    swapped = jnp.where(lane < 96, pltpu.roll(kt, 96, 1), pltpu.roll(kt, 32, 1))
    kp2 = jnp.where(lane >= 64, kt + swapped, 0.0)
    kk = jnp.dot(ckvn, wuk_ref[...], preferred_element_type=F32)
    for hd in range(C_HEADS):
        sl = slice(hd * LANES, (hd + 1) * LANES)
        kc_ref[hd] = (kk[:, sl] + kp2).astype(BF16)

    vtc_ref[...] = _nt_dot(wuvt_ref[...], ckvn).astype(BF16)


def _proj(x2d, g, wqk, colscale, wvt, wc, qn, kvn, wuq, wuk, wuvt, qtab, ktab, batch, tm):
    m = x2d.shape[0]
    per_seq = SEQ // tm
    row = lambda w: pl.BlockSpec((tm, w), lambda i: (i, 0))
    pos = pl.BlockSpec((tm, LANES), lambda i: (i % per_seq, 0))
    vt_spec = lambda w: pl.BlockSpec((None, w, tm), lambda i: (i // per_seq, 0, i % per_seq))
    heads_spec = lambda nh: pl.BlockSpec((nh, tm, LANES), lambda i: (0, i, 0))
    return pl.pallas_call(
        _proj_kernel,
        grid=(m // tm,),
        in_specs=[
            row(D_MODEL),
            _resident((1, D_MODEL)),
            _resident((D_MODEL, QK_W)),
            _resident((1, QK_W)),
            _resident((A_W + B_W, D_MODEL)),
            _resident((D_MODEL, C_CW)),
            _resident((1, C_Q_LORA)),
            _resident((1, C_KV_LORA)),
            _resident((C_Q_LORA, C_HEADS * LANES)),
            _resident((C_KV_LORA, C_HEADS * LANES)),
            _resident((C_OUT_W, C_KV_LORA)),
            pos, pos,
        ],
        out_specs=[heads_spec(QK_W // LANES), vt_spec(A_W), vt_spec(B_W), heads_spec(C_HEADS),
                   heads_spec(C_HEADS), vt_spec(C_OUT_W)],
        out_shape=[
            jax.ShapeDtypeStruct((QK_W // LANES, m, LANES), BF16),
            jax.ShapeDtypeStruct((batch, A_W, SEQ), BF16),
            jax.ShapeDtypeStruct((batch, B_W, SEQ), BF16),
            jax.ShapeDtypeStruct((C_HEADS, m, LANES), BF16),
            jax.ShapeDtypeStruct((C_HEADS, m, LANES), BF16),
            jax.ShapeDtypeStruct((batch, C_OUT_W, SEQ), BF16),
        ],
        compiler_params=pltpu.CompilerParams(
            dimension_semantics=("arbitrary",), vmem_limit_bytes=VMEM_LIMIT),
        name="proj",
    )(x2d, g, wqk, colscale, wvt, wc, qn, kvn, wuq, wuk, wuvt, qtab, ktab)


NQ_ATT = SEQ // T_ATT
KPOS_TERMS = 3
SUM_ROWS = 16


def _key_tile(jt):
    return slice(jt * T_ATT, (jt + 1) * T_ATT)


def _attend(qs_of, k_rows, vt_cols, tab_ref, bias_off_diag, finish):
    scored = None
    probs = None
    for t in range(NQ_ATT + 2):
        new_scored = None
        if t < NQ_ATT:
            n = (t + 1) * T_ATT
            new_scored = []
            for s, q in enumerate(qs_of(t)):
                st = _nt_dot(k_rows(s, n), q)
                if bias_off_diag:
                    st = st + tab_ref[tab_ref.shape[0] - n:, :]
                elif t == 0:
                    st = st + tab_ref[...]
                else:
                    st = jnp.concatenate([st[:n - T_ATT], st[n - T_ATT:] + tab_ref[...]], axis=0)
                new_scored.append((st, jnp.max(st, axis=0, keepdims=True)))
        new_probs = None
        if scored is not None:
            new_probs = [jnp.exp2(st - m).astype(BF16) for st, m in scored]
        if probs is not None:
            n = (t - 1) * T_ATT
            ones_rows = (lax.broadcasted_iota(jnp.int32, (SUM_ROWS, n), 0) == 0).astype(BF16)
            outs = []
            for s, p in enumerate(probs):
                vt = jnp.concatenate([vt_cols(s, n), ones_rows], axis=0)
                a = jnp.dot(vt, p, preferred_element_type=F32)
                outs.append((a[:-SUM_ROWS], a[-SUM_ROWS:-SUM_ROWS + 1]))
            finish(t - 2, outs)
        scored, probs = new_scored, new_probs


def _diff_attn_kernel(lam_init, q_ref, k_ref, vt_ref, tab_ref, kpos_ref, dl_ref, gain_ref, o_ref):
    dl = dl_ref[...]
    lam = (jnp.exp(jnp.sum(dl[0:1] * dl[1:2], axis=-1, keepdims=True))
           - jnp.exp(jnp.sum(dl[2:3] * dl[3:4], axis=-1, keepdims=True)) + lam_init)
    gain = gain_ref[...] * (1.0 - lam_init)
    lane = lax.broadcasted_iota(jnp.int32, (T_ATT, LANES), 1)
    q_ones = (lane < KPOS_TERMS).astype(BF16)

    def qs_of(qi):
        q = q_ref[_key_tile(qi), :]
        zero = jnp.zeros_like(q)
        return [jnp.concatenate([jnp.where(lane < A_HEAD_DIM, q, zero), q_ones], axis=1),
                jnp.concatenate([jnp.where(lane >= A_HEAD_DIM, q, zero), q_ones], axis=1)]

    def finish(qi, outs):
        (a0, l0), (a1, l1) = outs
        o = a0 / l0 - lam * (a1 / l1)
        y = o * lax.rsqrt(jnp.mean(o * o, axis=0, keepdims=True) + NORM_EPS) * gain
        o_ref[_key_tile(qi), :] = y.T.astype(BF16)

    _attend(qs_of, lambda s, n: jnp.concatenate([k_ref[:n, :], kpos_ref[:n, :]], axis=1),
            lambda s, n: vt_ref[:, :n], tab_ref, False, finish)


def _single_attn_kernel(q_ref, k_ref, vt_ref, tab_ref, o_ref):
    def finish(qi, outs):
        (a, l), = outs
        o_ref[_key_tile(qi), :] = (a / l).T.astype(BF16)

    _attend(lambda qi: [q_ref[_key_tile(qi), :]], lambda s, n: k_ref[:n, :],
            lambda s, n: vt_ref[:, :n], tab_ref, True, finish)


def _pair_attn_kernel(q_ref, k_ref, vt_ref, tab_ref, o_ref):
    def finish(qi, outs):
        (a0, l0), (a1, l1) = outs
        o_ref[_key_tile(qi), :] = jnp.concatenate([a0 / l0, a1 / l1], axis=0).T.astype(BF16)

    _attend(lambda qi: [q_ref[0, _key_tile(qi), :], q_ref[1, _key_tile(qi), :]],
            lambda s, n: k_ref[s, :n, :],
            lambda s, n: vt_ref[s * C_V_DIM:(s + 1) * C_V_DIM, :n],
            tab_ref, False, finish)


def _flash_call(kernel, name, q_arr, q_head0, k_arr, k_head0, vt_arr, tab, extra, extra_specs,
                batch, steps, heads_per_step, tab_per_head):
    m = q_arr.shape[1]
    vrows = vt_arr.shape[1] // steps
    if heads_per_step == 1:
        qk_spec = lambda h0: pl.BlockSpec((None, SEQ, LANES), lambda h, b: (h0 + h, b, 0))
    else:
        qk_spec = lambda h0: pl.BlockSpec((heads_per_step, SEQ, LANES),
                                          lambda h, b: (h0 // heads_per_step + h, b, 0))
    in_specs = [
        qk_spec(q_head0),
        qk_spec(k_head0),
        pl.BlockSpec((None, vrows, SEQ), lambda h, b: (b, h, 0)),
        pl.BlockSpec((None,) + tab.shape[1:],
                     (lambda h, b: (h, 0, 0)) if tab_per_head else (lambda h, b: (0, 0, 0))),
    ] + extra_specs
    return pl.pallas_call(
        kernel,
        grid=(steps, batch),
        in_specs=in_specs,
        out_specs=pl.BlockSpec((None, SEQ, LANES), lambda h, b: (h, b, 0)),
        out_shape=jax.ShapeDtypeStruct((steps, m, LANES), BF16),
        compiler_params=pltpu.CompilerParams(
            dimension_semantics=("arbitrary", "arbitrary"), vmem_limit_bytes=VMEM_LIMIT),
        name=name,
    )(q_arr, k_arr, vt_arr, tab, *extra)


def _mix_kernel(x_ref, g_ref, oa_ref, ob_ref, oc_ref, wg_ref, wa_ref, wb_ref, wc_ref, wo_ref, o_ref):
    x = x_ref[...]
    h = _rms(x, g_ref[...]).astype(BF16)
    mix = None
    for n, (o_br, w_br) in enumerate(((oa_ref, wa_ref), (ob_ref, wb_ref), (oc_ref, wc_ref))):
        gate = jax.nn.sigmoid(jnp.dot(h, wg_ref[:, n * D_MODEL:(n + 1) * D_MODEL], preferred_element_type=F32))
        o = jnp.concatenate([o_br[hd] for hd in range(o_br.shape[0])], axis=1)
        term = gate * jnp.dot(o, w_br[...], preferred_element_type=F32)
        mix = term if mix is None else mix + term
    o_ref[...] = x + jnp.dot(mix.astype(BF16), wo_ref[...], preferred_element_type=F32)


def _mix(x2d, g, oa, ob, oc, wg, wa, wb, wc, wo, tm):
    m = x2d.shape[0]
    row = lambda w: pl.BlockSpec((tm, w), lambda i: (i, 0))
    heads = lambda w: pl.BlockSpec((w // LANES, tm, LANES), lambda i: (0, i, 0))
    return pl.pallas_call(
        _mix_kernel,
        grid=(m // tm,),
        in_specs=[row(D_MODEL), _resident((1, D_MODEL)), heads(A_W), heads(B_W), heads(C_OUT_W),
                  _resident((D_MODEL, 3 * D_MODEL)), _resident((A_W, D_MODEL)), _resident((B_W, D_MODEL)),
                  _resident((C_OUT_W, D_MODEL)), _resident((D_MODEL, D_MODEL))],
        out_specs=row(D_MODEL),
        out_shape=jax.ShapeDtypeStruct((m, D_MODEL), F32),
        compiler_params=pltpu.CompilerParams(
            dimension_semantics=("arbitrary",), vmem_limit_bytes=VMEM_LIMIT),
        name="mix",
    )(x2d, g, oa, ob, oc, wg, wa, wb, wc, wo)


def _ffn_kernel(final, x_ref, g_ref, wg_ref, wu_ref, wd_ref, fg_ref, o_ref):
    x = x_ref[...]
    h = _rms(x, g_ref[...]).astype(BF16)
    gate = jnp.dot(h, wg_ref[...], preferred_element_type=F32)
    up = jnp.dot(h, wu_ref[...], preferred_element_type=F32)
    act = (gate * jax.nn.sigmoid(gate) * up).astype(BF16)
    y = x + jnp.dot(act, wd_ref[...], preferred_element_type=F32)
    o_ref[...] = _rms(y, fg_ref[...]) if final else y


def _ffn(x2d, g, wg, wu, wd, fg, final, tm):
    m = x2d.shape[0]
    return pl.pallas_call(
        functools.partial(_ffn_kernel, final),
        grid=(m // tm,),
        in_specs=[
            pl.BlockSpec((tm, D_MODEL), lambda i: (i, 0)),
            _resident((1, D_MODEL)),
            _resident((D_MODEL, FFN_HIDDEN)),
            _resident((D_MODEL, FFN_HIDDEN)),
            _resident((FFN_HIDDEN, D_MODEL)),
            _resident((1, D_MODEL)),
        ],
        out_specs=pl.BlockSpec((tm, D_MODEL), lambda i: (i, 0)),
        out_shape=jax.ShapeDtypeStruct((m, D_MODEL), F32),
        compiler_params=pltpu.CompilerParams(
            dimension_semantics=("arbitrary",), vmem_limit_bytes=VMEM_LIMIT),
        name="ffn",
    )(x2d, g, wg, wu, wd, fg)


def _alibi_slopes():
    n = A_HEADS + B_HEADS
    s = 2.0 ** (-8.0 * np.arange(1, n + 1, dtype=np.float64) / n)
    return s[0::2], s[1::2]


def _bias_tables(slopes, multiplicity, rows):
    r = jnp.arange(rows, dtype=jnp.int32)[:, None]
    i = jnp.arange(T_ATT, dtype=jnp.int32)[None, :]
    d = (rows - T_ATT) - r + i
    mult = multiplicity(d)
    valid = (d >= 0) & (mult > 0)
    logm = jnp.log2(jnp.maximum(mult, 1).astype(F32))
    sl = jnp.asarray(np.asarray(slopes) * LOG2E, F32)[:, None, None]
    tab = logm[None] - sl * d.astype(F32)[None]
    return jnp.where(valid[None], tab, NEG_INF)


def _kpos_tables(slopes):
    v = (np.asarray(slopes) * LOG2E).astype(np.float32)[:, None] * np.arange(SEQ, dtype=np.float32)[None, :]
    tab = np.zeros(v.shape + (LANES,), BF16)
    for term in range(KPOS_TERMS):
        piece = v.astype(BF16)
        tab[..., term] = piece
        v = v - piece.astype(np.float32)
    return jnp.asarray(tab)


def _dilated_multiplicity(d):
    mult = jnp.zeros_like(d)
    for window, dil in B_PATTERNS:
        mult = mult + ((d % dil == 0) & (d // dil <= window // dil)).astype(jnp.int32)
    return mult


def _rope_tables():
    half = C_ROPE_DIM // 2
    inv_freq = ROPE_THETA ** (-jnp.arange(half, dtype=F32) / half)
    ang = jnp.arange(SEQ, dtype=F32)[:, None] * inv_freq[None, :]
    cos = jnp.concatenate([jnp.cos(ang)] * 2, axis=-1)
    sin = jnp.concatenate([jnp.sin(ang)] * 2, axis=-1)
    scale = (C_NOPE_DIM + C_ROPE_DIM) ** -0.5 * LOG2E
    qtab = jnp.concatenate([jnp.full((SEQ, C_NOPE_DIM), scale, F32), cos * scale, sin * scale], axis=-1)
    ktab = jnp.concatenate([jnp.zeros((SEQ, C_NOPE_DIM), F32), cos, sin], axis=-1)
    return qtab, ktab


def _rotate_half_cols(w):
    half = C_ROPE_DIM // 2
    return jnp.concatenate([-w[..., half:], w[..., :half]], axis=-1)


def _layer_weights(w_in, w_uq, w_ukv):
    sa, sb = A_HEAD_DIM ** -0.5 * LOG2E, B_HEAD_DIM ** -0.5 * LOG2E
    b0 = 3 * A_W
    w_qk = jnp.concatenate([w_in[:, :2 * A_W], w_in[:, b0:b0 + 2 * B_W]], axis=1).astype(BF16)
    w_gate = w_in[:, GATE_OFF:].astype(BF16)
    colscale = jnp.concatenate(
        [jnp.full((A_W,), sa, F32), jnp.ones((A_W,), F32), jnp.full((B_W,), sb, F32),
         jnp.ones((B_W,), F32)])[None, :]
    w_vt = jnp.concatenate([w_in[:, 2 * A_W:3 * A_W], w_in[:, b0 + 2 * B_W:b0 + 3 * B_W]], axis=1).T.astype(BF16)
    w_pe = w_in[:, C_PE_OFF:GATE_OFF]
    w_c = jnp.concatenate(
        [w_in[:, C_Q_OFF:C_PE_OFF], jnp.zeros((D_MODEL, C_NOPE_DIM), F32), w_pe, _rotate_half_cols(w_pe)],
        axis=1).astype(BF16)
    uq = w_uq.reshape(C_Q_LORA, C_HEADS, C_NOPE_DIM + C_ROPE_DIM)
    uq_pe = uq[..., C_NOPE_DIM:]
    w_uq_ext = jnp.concatenate([uq, _rotate_half_cols(uq_pe)], axis=-1).reshape(C_Q_LORA, C_HEADS * LANES).astype(BF16)
    ukv = w_ukv.reshape(C_KV_LORA, C_HEADS, C_NOPE_DIM + C_V_DIM)
    w_uk = jnp.concatenate([ukv[..., :C_NOPE_DIM], jnp.zeros((C_KV_LORA, C_HEADS, LANES - C_NOPE_DIM), F32)],
                           axis=-1).reshape(C_KV_LORA, C_HEADS * LANES).astype(BF16)
    w_uvt = ukv[..., C_NOPE_DIM:].reshape(C_KV_LORA, C_OUT_W).T.astype(BF16)
    return w_qk, colscale, w_gate, w_vt, w_c, w_uq_ext, w_uk, w_uvt


def kernel(x, attn_norm, w_in, diff_lambda, diff_norm, mla_q_norm, mla_w_uq, mla_kv_norm, mla_w_ukv,
           w_branch_a, w_branch_b, w_branch_c, w_out, ffn_norm, w_ffn_gate, w_ffn_up, w_ffn_down,
           final_norm):
    batch, seq, d_model = x.shape
    assert (seq, d_model) == (SEQ, D_MODEL)
    m = batch * seq
    tm_tok = 512

    slopes_a, slopes_b = _alibi_slopes()
    kpos_a = _kpos_tables(slopes_a)
    tab_b = _bias_tables(slopes_b, _dilated_multiplicity, SEQ)
    tab_c = _bias_tables(np.zeros((1,)), lambda d: jnp.ones_like(d), T_ATT)
    qtab, ktab = _rope_tables()

    x2d = x.reshape(m, D_MODEL)
    for l in range(DEPTH):
        w_qk, colscale, w_gate, w_vt, w_c, w_uq_ext, w_uk, w_uvt = _layer_weights(
            w_in[l], mla_w_uq[l], mla_w_ukv[l])
        g_attn = attn_norm[l][None, :]
        qk, vta, vtb, qc, kc, vtc = _proj(
            x2d, g_attn, w_qk, colscale, w_vt, w_c, mla_q_norm[l][None, :], mla_kv_norm[l][None, :],
            w_uq_ext, w_uk, w_uvt, qtab, ktab, batch, tm_tok)

        lam_init = 0.8 - 0.6 * math.exp(-0.3 * l)
        oa = _flash_call(
            functools.partial(_diff_attn_kernel, lam_init), "diff_attn", qk, 0, qk, A_HEADS, vta, tab_c,
            [kpos_a, diff_lambda[l], diff_norm[l][:, None]],
            [pl.BlockSpec((None, SEQ, LANES), lambda h, b: (h, 0, 0)),
             pl.BlockSpec((4, A_HEAD_DIM), lambda h, b: (0, 0)),
             pl.BlockSpec((2 * A_HEAD_DIM, 1), lambda h, b: (0, 0))],
            batch, A_HEADS, 1, False)
        ob = _flash_call(_single_attn_kernel, "dilated_attn", qk, 2 * A_HEADS, qk, 2 * A_HEADS + B_HEADS, vtb, tab_b,
                         [], [], batch, B_HEADS, 1, True)
        oc = _flash_call(_pair_attn_kernel, "mla_attn", qc, 0, kc, 0, vtc, tab_c,
                         [], [], batch, C_HEADS // 2, 2, False)

        x2d = _mix(x2d, g_attn, oa, ob, oc, w_gate, w_branch_a[l].astype(BF16), w_branch_b[l].astype(BF16),
                   w_branch_c[l].astype(BF16), w_out[l].astype(BF16), tm_tok)
        x2d = _ffn(x2d, ffn_norm[l][None, :], w_ffn_gate[l].astype(BF16), w_ffn_up[l].astype(BF16),
                   w_ffn_down[l].astype(BF16), final_norm[None, :], l == DEPTH - 1, tm_tok)
    return x2d.reshape(batch, seq, d_model)
```

```python
import functools
import math

import jax
import jax.numpy as jnp
import numpy as np
from jax import lax
from jax.experimental import pallas as pl
from jax.experimental.pallas import tpu as pltpu

D_MODEL = 1024
SEQ = 2048
DEPTH = 2
A_HEADS = 4
A_HEAD_DIM = 64
B_HEADS = 4
B_HEAD_DIM = 128
B_PATTERNS = ((128, 1), (512, 4), (2048, 16))
C_HEADS = 8
C_NOPE_DIM = 64
C_ROPE_DIM = 32
C_V_DIM = 64
C_Q_LORA = 384
C_KV_LORA = 256
ROPE_THETA = 10000.0
FFN_HIDDEN = 2816
A_W = A_HEADS * 2 * A_HEAD_DIM
B_W = B_HEADS * B_HEAD_DIM
C_OUT_W = C_HEADS * C_V_DIM
C_Q_OFF = 2 * A_W + A_W + 3 * B_W
C_KV_OFF = C_Q_OFF + C_Q_LORA
C_PE_OFF = C_KV_OFF + C_KV_LORA
GATE_OFF = C_PE_OFF + C_ROPE_DIM
NORM_EPS = 1e-6
NEG_INF = -1e30
LOG2E = 1.4426950408889634

LANES = 128
T_ATT = 256
VMEM_LIMIT = 48 * 1024 * 1024

BF16 = jnp.bfloat16
F32 = jnp.float32


def _rms(xf, g):
    return xf * lax.rsqrt(jnp.mean(xf * xf, axis=-1, keepdims=True) + NORM_EPS) * g


def _nt_dot(a, b):
    return lax.dot_general(a, b, (((1,), (1,)), ((), ())), preferred_element_type=F32)


QK_W = 2 * A_W + 2 * B_W
C_CW = 768
C_PE_GROUP = C_Q_LORA + C_KV_LORA


def _resident(shape):
    return pl.BlockSpec(shape, lambda *_: (0,) * len(shape), pipeline_mode=pl.Buffered(1))


def _proj_kernel(x_ref, g_ref, wqk_ref, cs_ref, wvt_ref, wc_ref, qn_ref, kvn_ref, wuq_ref, wuk_ref,
                 wuvt_ref, qtab_ref, ktab_ref, qk_ref, vta_ref, vtb_ref, qc_ref, kc_ref, vtc_ref):
    h = _rms(x_ref[...], g_ref[...]).astype(BF16)

    qk = (jnp.dot(h, wqk_ref[...], preferred_element_type=F32) * cs_ref[...]).astype(BF16)
    for hd in range(QK_W // LANES):
        qk_ref[hd] = qk[:, hd * LANES:(hd + 1) * LANES]

    vt = _nt_dot(wvt_ref[...], h).astype(BF16)
    vta_ref[...] = vt[:A_W]
    vtb_ref[...] = vt[A_W:]

    c = jnp.dot(h, wc_ref[...], preferred_element_type=F32)
    cqn = _rms(c[:, :C_Q_LORA], qn_ref[...]).astype(BF16)
    ckvn = _rms(c[:, C_Q_LORA:C_PE_GROUP], kvn_ref[...]).astype(BF16)

    q = jnp.dot(cqn, wuq_ref[...], preferred_element_type=F32)
    qtab = qtab_ref[...]
    for hd in range(C_HEADS):
        sl = slice(hd * LANES, (hd + 1) * LANES)
        qc_ref[hd] = (q[:, sl] * qtab).astype(BF16)

    kt = c[:, C_PE_GROUP:] * ktab_ref[...]
    lane = lax.broadcasted_iota(jnp.int32, kt.shape, 1)
    swapped = jnp.where(lane < 96, pltpu.roll(kt, 96, 1), pltpu.roll(kt, 32, 1))
    kp2 = jnp.where(lane >= 64, kt + swapped, 0.0)
    kk = jnp.dot(ckvn, wuk_ref[...], preferred_element_type=F32)
    for hd in range(C_HEADS):
        sl = slice(hd * LANES, (hd + 1) * LANES)
        kc_ref[hd] = (kk[:, sl] + kp2).astype(BF16)

    vtc_ref[...] = _nt_dot(wuvt_ref[...], ckvn).astype(BF16)


def _proj(x2d, g, wqk, colscale, wvt, wc, qn, kvn, wuq, wuk, wuvt, qtab, ktab, batch, tm):
    m = x2d.shape[0]
    per_seq = SEQ // tm
    row = lambda w: pl.BlockSpec((tm, w), lambda i: (i, 0))
    pos = pl.BlockSpec((tm, LANES), lambda i: (i % per_seq, 0))
    vt_spec = lambda w: pl.BlockSpec((None, w, tm), lambda i: (i // per_seq, 0, i % per_seq))
    heads_spec = lambda nh: pl.BlockSpec((nh, tm, LANES), lambda i: (0, i, 0))
    return pl.pallas_call(
        _proj_kernel,
        grid=(m // tm,),
        in_specs=[
            row(D_MODEL),
            _resident((1, D_MODEL)),
            _resident((D_MODEL, QK_W)),
            _resident((1, QK_W)),
            _resident((A_W + B_W, D_MODEL)),
            _resident((D_MODEL, C_CW)),
            _resident((1, C_Q_LORA)),
            _resident((1, C_KV_LORA)),
            _resident((C_Q_LORA, C_HEADS * LANES)),
            _resident((C_KV_LORA, C_HEADS * LANES)),
            _resident((C_OUT_W, C_KV_LORA)),
            pos, pos,
        ],
        out_specs=[heads_spec(QK_W // LANES), vt_spec(A_W), vt_spec(B_W), heads_spec(C_HEADS),
                   heads_spec(C_HEADS), vt_spec(C_OUT_W)],
        out_shape=[
            jax.ShapeDtypeStruct((QK_W // LANES, m, LANES), BF16),
            jax.ShapeDtypeStruct((batch, A_W, SEQ), BF16),
            jax.ShapeDtypeStruct((batch, B_W, SEQ), BF16),
            jax.ShapeDtypeStruct((C_HEADS, m, LANES), BF16),
            jax.ShapeDtypeStruct((C_HEADS, m, LANES), BF16),
            jax.ShapeDtypeStruct((batch, C_OUT_W, SEQ), BF16),
        ],
        compiler_params=pltpu.CompilerParams(
            dimension_semantics=("arbitrary",), vmem_limit_bytes=VMEM_LIMIT),
        name="proj",
    )(x2d, g, wqk, colscale, wvt, wc, qn, kvn, wuq, wuk, wuvt, qtab, ktab)


NQ_ATT = SEQ // T_ATT
KPOS_TERMS = 3
SUM_ROWS = 16


def _key_tile(jt):
    return slice(jt * T_ATT, (jt + 1) * T_ATT)


def _causal_exp2(st, m):
    n, half = st.shape[0], T_ATT // 2
    live = jnp.exp2(st[:n - half] - m).astype(BF16)
    corner = jnp.exp2(st[n - half:, half:] - m[:, half:]).astype(BF16)
    dead = jnp.zeros((half, half), BF16)
    return jnp.concatenate([live, jnp.concatenate([dead, corner], axis=1)], axis=0)


def _attend(qs_of, k_rows, vt_cols, tab_ref, bias_off_diag, finish):
    scored = None
    probs = None
    for t in range(NQ_ATT + 2):
        new_scored = None
        if t < NQ_ATT:
            n = (t + 1) * T_ATT
            new_scored = []
            for s, q in enumerate(qs_of(t)):
                st = _nt_dot(k_rows(s, n), q)
                if bias_off_diag:
                    st = st + tab_ref[tab_ref.shape[0] - n:, :]
                elif t == 0:
                    st = st + tab_ref[...]
                else:
                    st = jnp.concatenate([st[:n - T_ATT], st[n - T_ATT:] + tab_ref[...]], axis=0)
                new_scored.append((st, jnp.max(st, axis=0, keepdims=True)))
        new_probs = None
        if scored is not None:
            new_probs = [_causal_exp2(st, m) for st, m in scored]
        if probs is not None:
            n = (t - 1) * T_ATT
            ones_rows = (lax.broadcasted_iota(jnp.int32, (SUM_ROWS, n), 0) == 0).astype(BF16)
            outs = []
            for s, p in enumerate(probs):
                vt = jnp.concatenate([vt_cols(s, n), ones_rows], axis=0)
                a = jnp.dot(vt, p, preferred_element_type=F32)
                outs.append((a[:-SUM_ROWS], a[-SUM_ROWS:-SUM_ROWS + 1]))
            finish(t - 2, outs)
        scored, probs = new_scored, new_probs


def _diff_attn_kernel(lam_init, q_ref, k_ref, vt_ref, tab_ref, kpos_ref, dl_ref, gain_ref, o_ref):
    dl = dl_ref[...]
    lam = (jnp.exp(jnp.sum(dl[0:1] * dl[1:2], axis=-1, keepdims=True))
           - jnp.exp(jnp.sum(dl[2:3] * dl[3:4], axis=-1, keepdims=True)) + lam_init)
    gain = gain_ref[...] * (1.0 - lam_init)
    lane = lax.broadcasted_iota(jnp.int32, (T_ATT, LANES), 1)
    q_ones = (lane < KPOS_TERMS).astype(BF16)

    def qs_of(qi):
        q = q_ref[_key_tile(qi), :]
        zero = jnp.zeros_like(q)
        return [jnp.concatenate([jnp.where(lane < A_HEAD_DIM, q, zero), q_ones], axis=1),
                jnp.concatenate([jnp.where(lane >= A_HEAD_DIM, q, zero), q_ones], axis=1)]

    def finish(qi, outs):
        (a0, l0), (a1, l1) = outs
        o = a0 / l0 - lam * (a1 / l1)
        y = o * lax.rsqrt(jnp.mean(o * o, axis=0, keepdims=True) + NORM_EPS) * gain
        o_ref[_key_tile(qi), :] = y.T.astype(BF16)

    _attend(qs_of, lambda s, n: jnp.concatenate([k_ref[:n, :], kpos_ref[:n, :]], axis=1),
            lambda s, n: vt_ref[:, :n], tab_ref, False, finish)


def _single_attn_kernel(q_ref, k_ref, vt_ref, tab_ref, o_ref):
    def finish(qi, outs):
        (a, l), = outs
        o_ref[_key_tile(qi), :] = (a / l).T.astype(BF16)

    _attend(lambda qi: [q_ref[_key_tile(qi), :]], lambda s, n: k_ref[:n, :],
            lambda s, n: vt_ref[:, :n], tab_ref, True, finish)


def _pair_attn_kernel(q_ref, k_ref, vt_ref, tab_ref, o_ref):
    def finish(qi, outs):
        (a0, l0), (a1, l1) = outs
        o_ref[_key_tile(qi), :] = jnp.concatenate([a0 / l0, a1 / l1], axis=0).T.astype(BF16)

    _attend(lambda qi: [q_ref[0, _key_tile(qi), :], q_ref[1, _key_tile(qi), :]],
            lambda s, n: k_ref[s, :n, :],
            lambda s, n: vt_ref[s * C_V_DIM:(s + 1) * C_V_DIM, :n],
            tab_ref, False, finish)


def _flash_call(kernel, name, q_arr, q_head0, k_arr, k_head0, vt_arr, tab, extra, extra_specs,
                batch, steps, heads_per_step, tab_per_head):
    m = q_arr.shape[1]
    vrows = vt_arr.shape[1] // steps
    if heads_per_step == 1:
        qk_spec = lambda h0: pl.BlockSpec((None, SEQ, LANES), lambda h, b: (h0 + h, b, 0))
    else:
        qk_spec = lambda h0: pl.BlockSpec((heads_per_step, SEQ, LANES),
                                          lambda h, b: (h0 // heads_per_step + h, b, 0))
    in_specs = [
        qk_spec(q_head0),
        qk_spec(k_head0),
        pl.BlockSpec((None, vrows, SEQ), lambda h, b: (b, h, 0)),
        pl.BlockSpec((None,) + tab.shape[1:],
                     (lambda h, b: (h, 0, 0)) if tab_per_head else (lambda h, b: (0, 0, 0))),
    ] + extra_specs
    return pl.pallas_call(
        kernel,
        grid=(steps, batch),
        in_specs=in_specs,
        out_specs=pl.BlockSpec((None, SEQ, LANES), lambda h, b: (h, b, 0)),
        out_shape=jax.ShapeDtypeStruct((steps, m, LANES), BF16),
        compiler_params=pltpu.CompilerParams(
            dimension_semantics=("arbitrary", "arbitrary"), vmem_limit_bytes=VMEM_LIMIT),
        name=name,
    )(q_arr, k_arr, vt_arr, tab, *extra)


def _mix_kernel(x_ref, g_ref, oa_ref, ob_ref, oc_ref, wg_ref, wa_ref, wb_ref, wc_ref, wo_ref, o_ref):
    x = x_ref[...]
    h = _rms(x, g_ref[...]).astype(BF16)
    mix = None
    for n, (o_br, w_br) in enumerate(((oa_ref, wa_ref), (ob_ref, wb_ref), (oc_ref, wc_ref))):
        gate = jax.nn.sigmoid(jnp.dot(h, wg_ref[:, n * D_MODEL:(n + 1) * D_MODEL], preferred_element_type=F32))
        o = jnp.concatenate([o_br[hd] for hd in range(o_br.shape[0])], axis=1)
        term = gate * jnp.dot(o, w_br[...], preferred_element_type=F32)
        mix = term if mix is None else mix + term
    o_ref[...] = x + jnp.dot(mix.astype(BF16), wo_ref[...], preferred_element_type=F32)


def _mix(x2d, g, oa, ob, oc, wg, wa, wb, wc, wo, tm):
    m = x2d.shape[0]
    row = lambda w: pl.BlockSpec((tm, w), lambda i: (i, 0))
    heads = lambda w: pl.BlockSpec((w // LANES, tm, LANES), lambda i: (0, i, 0))
    return pl.pallas_call(
        _mix_kernel,
        grid=(m // tm,),
        in_specs=[row(D_MODEL), _resident((1, D_MODEL)), heads(A_W), heads(B_W), heads(C_OUT_W),
                  _resident((D_MODEL, 3 * D_MODEL)), _resident((A_W, D_MODEL)), _resident((B_W, D_MODEL)),
                  _resident((C_OUT_W, D_MODEL)), _resident((D_MODEL, D_MODEL))],
        out_specs=row(D_MODEL),
        out_shape=jax.ShapeDtypeStruct((m, D_MODEL), F32),
        compiler_params=pltpu.CompilerParams(
            dimension_semantics=("arbitrary",), vmem_limit_bytes=VMEM_LIMIT),
        name="mix",
    )(x2d, g, oa, ob, oc, wg, wa, wb, wc, wo)


def _ffn_kernel(final, x_ref, g_ref, wg_ref, wu_ref, wd_ref, fg_ref, o_ref):
    x = x_ref[...]
    h = _rms(x, g_ref[...]).astype(BF16)
    gate = jnp.dot(h, wg_ref[...], preferred_element_type=F32)
    up = jnp.dot(h, wu_ref[...], preferred_element_type=F32)
    act = (gate * jax.nn.sigmoid(gate) * up).astype(BF16)
    y = x + jnp.dot(act, wd_ref[...], preferred_element_type=F32)
    o_ref[...] = _rms(y, fg_ref[...]) if final else y


def _ffn(x2d, g, wg, wu, wd, fg, final, tm):
    m = x2d.shape[0]
    return pl.pallas_call(
        functools.partial(_ffn_kernel, final),
        grid=(m // tm,),
        in_specs=[
            pl.BlockSpec((tm, D_MODEL), lambda i: (i, 0)),
            _resident((1, D_MODEL)),
            _resident((D_MODEL, FFN_HIDDEN)),
            _resident((D_MODEL, FFN_HIDDEN)),
            _resident((FFN_HIDDEN, D_MODEL)),
            _resident((1, D_MODEL)),
        ],
        out_specs=pl.BlockSpec((tm, D_MODEL), lambda i: (i, 0)),
        out_shape=jax.ShapeDtypeStruct((m, D_MODEL), F32),
        compiler_params=pltpu.CompilerParams(
            dimension_semantics=("arbitrary",), vmem_limit_bytes=VMEM_LIMIT),
        name="ffn",
    )(x2d, g, wg, wu, wd, fg)


def _alibi_slopes():
    n = A_HEADS + B_HEADS
    s = 2.0 ** (-8.0 * np.arange(1, n + 1, dtype=np.float64) / n)
    return s[0::2], s[1::2]


def _bias_tables(slopes, multiplicity, rows):
    r = np.arange(rows, dtype=np.int64)[:, None]
    i = np.arange(T_ATT, dtype=np.int64)[None, :]
    d = (rows - T_ATT) - r + i
    mult = multiplicity(d)
    valid = (d >= 0) & (mult > 0)
    tab = np.log2(np.maximum(mult, 1))[None] - (np.asarray(slopes) * LOG2E)[:, None, None] * d[None]
    return jnp.asarray(np.where(valid[None], tab, NEG_INF).astype(np.float32))


def _kpos_tables(slopes):
    v = (np.asarray(slopes) * LOG2E).astype(np.float32)[:, None] * np.arange(SEQ, dtype=np.float32)[None, :]
    tab = np.zeros(v.shape + (LANES,), BF16)
    for term in range(KPOS_TERMS):
        piece = v.astype(BF16)
        tab[..., term] = piece
        v = v - piece.astype(np.float32)
    return jnp.asarray(tab)


def _dilated_multiplicity(d):
    mult = np.zeros_like(d)
    for window, dil in B_PATTERNS:
        mult = mult + ((d % dil == 0) & (d // dil <= window // dil))
    return mult


def _rope_tables():
    half = C_ROPE_DIM // 2
    inv_freq = ROPE_THETA ** (-np.arange(half, dtype=np.float64) / half)
    ang = np.arange(SEQ, dtype=np.float64)[:, None] * inv_freq[None, :]
    cos = np.concatenate([np.cos(ang)] * 2, axis=-1)
    sin = np.concatenate([np.sin(ang)] * 2, axis=-1)
    scale = (C_NOPE_DIM + C_ROPE_DIM) ** -0.5 * LOG2E
    qtab = np.concatenate([np.full((SEQ, C_NOPE_DIM), scale), cos * scale, sin * scale], axis=-1)
    ktab = np.concatenate([np.zeros((SEQ, C_NOPE_DIM)), cos, sin], axis=-1)
    return jnp.asarray(qtab.astype(np.float32)), jnp.asarray(ktab.astype(np.float32))


def _rotate_half_cols(w):
    half = C_ROPE_DIM // 2
    return jnp.concatenate([-w[..., half:], w[..., :half]], axis=-1)


def _layer_weights(w_in, w_uq, w_ukv):
    sa, sb = A_HEAD_DIM ** -0.5 * LOG2E, B_HEAD_DIM ** -0.5 * LOG2E
    b0 = 3 * A_W
    w_qk = jnp.concatenate([w_in[:, :2 * A_W], w_in[:, b0:b0 + 2 * B_W]], axis=1).astype(BF16)
    w_gate = w_in[:, GATE_OFF:].astype(BF16)
    colscale = jnp.concatenate(
        [jnp.full((A_W,), sa, F32), jnp.ones((A_W,), F32), jnp.full((B_W,), sb, F32),
         jnp.ones((B_W,), F32)])[None, :]
    w_vt = jnp.concatenate([w_in[:, 2 * A_W:3 * A_W], w_in[:, b0 + 2 * B_W:b0 + 3 * B_W]], axis=1).T.astype(BF16)
    w_pe = w_in[:, C_PE_OFF:GATE_OFF]
    w_c = jnp.concatenate(
        [w_in[:, C_Q_OFF:C_PE_OFF], jnp.zeros((D_MODEL, C_NOPE_DIM), F32), w_pe, _rotate_half_cols(w_pe)],
        axis=1).astype(BF16)
    uq = w_uq.reshape(C_Q_LORA, C_HEADS, C_NOPE_DIM + C_ROPE_DIM)
    uq_pe = uq[..., C_NOPE_DIM:]
    w_uq_ext = jnp.concatenate([uq, _rotate_half_cols(uq_pe)], axis=-1).reshape(C_Q_LORA, C_HEADS * LANES).astype(BF16)
    ukv = w_ukv.reshape(C_KV_LORA, C_HEADS, C_NOPE_DIM + C_V_DIM)
    w_uk = jnp.concatenate([ukv[..., :C_NOPE_DIM], jnp.zeros((C_KV_LORA, C_HEADS, LANES - C_NOPE_DIM), F32)],
                           axis=-1).reshape(C_KV_LORA, C_HEADS * LANES).astype(BF16)
    w_uvt = ukv[..., C_NOPE_DIM:].reshape(C_KV_LORA, C_OUT_W).T.astype(BF16)
    return w_qk, colscale, w_gate, w_vt, w_c, w_uq_ext, w_uk, w_uvt


def kernel(x, attn_norm, w_in, diff_lambda, diff_norm, mla_q_norm, mla_w_uq, mla_kv_norm, mla_w_ukv,
           w_branch_a, w_branch_b, w_branch_c, w_out, ffn_norm, w_ffn_gate, w_ffn_up, w_ffn_down,
           final_norm):
    batch, seq, d_model = x.shape
    assert (seq, d_model) == (SEQ, D_MODEL)
    m = batch * seq
    tm_tok = 512

    slopes_a, slopes_b = _alibi_slopes()
    kpos_a = _kpos_tables(slopes_a)
    tab_b = _bias_tables(slopes_b, _dilated_multiplicity, SEQ)
    tab_c = _bias_tables(np.zeros((1,)), np.ones_like, T_ATT)
    qtab, ktab = _rope_tables()

    x2d = x.reshape(m, D_MODEL)
    for l in range(DEPTH):
        w_qk, colscale, w_gate, w_vt, w_c, w_uq_ext, w_uk, w_uvt = _layer_weights(
            w_in[l], mla_w_uq[l], mla_w_ukv[l])
        g_attn = attn_norm[l][None, :]
        qk, vta, vtb, qc, kc, vtc = _proj(
            x2d, g_attn, w_qk, colscale, w_vt, w_c, mla_q_norm[l][None, :], mla_kv_norm[l][None, :],
            w_uq_ext, w_uk, w_uvt, qtab, ktab, batch, tm_tok)

        lam_init = 0.8 - 0.6 * math.exp(-0.3 * l)
        oa = _flash_call(
            functools.partial(_diff_attn_kernel, lam_init), "diff_attn", qk, 0, qk, A_HEADS, vta, tab_c,
            [kpos_a, diff_lambda[l], diff_norm[l][:, None]],
            [pl.BlockSpec((None, SEQ, LANES), lambda h, b: (h, 0, 0)),
             pl.BlockSpec((4, A_HEAD_DIM), lambda h, b: (0, 0)),
             pl.BlockSpec((2 * A_HEAD_DIM, 1), lambda h, b: (0, 0))],
            batch, A_HEADS, 1, False)
        ob = _flash_call(_single_attn_kernel, "dilated_attn", qk, 2 * A_HEADS, qk, 2 * A_HEADS + B_HEADS, vtb, tab_b,
                         [], [], batch, B_HEADS, 1, True)
        oc = _flash_call(_pair_attn_kernel, "mla_attn", qc, 0, kc, 0, vtc, tab_c,
                         [], [], batch, C_HEADS // 2, 2, False)

        x2d = _mix(x2d, g_attn, oa, ob, oc, w_gate, w_branch_a[l].astype(BF16), w_branch_b[l].astype(BF16),
                   w_branch_c[l].astype(BF16), w_out[l].astype(BF16), tm_tok)
        x2d = _ffn(x2d, ffn_norm[l][None, :], w_ffn_gate[l].astype(BF16), w_ffn_up[l].astype(BF16),
                   w_ffn_down[l].astype(BF16), final_norm[None, :], l == DEPTH - 1, tm_tok)
    return x2d.reshape(batch, seq, d_model)
```

```python
import functools
import math

import jax
import jax.numpy as jnp
import numpy as np
from jax import lax
from jax.experimental import pallas as pl
from jax.experimental.pallas import tpu as pltpu

D_MODEL = 1024
SEQ = 2048
DEPTH = 2
A_HEADS = 4
A_HEAD_DIM = 64
B_HEADS = 4
B_HEAD_DIM = 128
B_PATTERNS = ((128, 1), (512, 4), (2048, 16))
C_HEADS = 8
C_NOPE_DIM = 64
C_ROPE_DIM = 32
C_V_DIM = 64
C_Q_LORA = 384
C_KV_LORA = 256
ROPE_THETA = 10000.0
FFN_HIDDEN = 2816
A_W = A_HEADS * 2 * A_HEAD_DIM
B_W = B_HEADS * B_HEAD_DIM
C_OUT_W = C_HEADS * C_V_DIM
C_Q_OFF = 2 * A_W + A_W + 3 * B_W
C_KV_OFF = C_Q_OFF + C_Q_LORA
C_PE_OFF = C_KV_OFF + C_KV_LORA
GATE_OFF = C_PE_OFF + C_ROPE_DIM
NORM_EPS = 1e-6
NEG_INF = -1e30
LOG2E = 1.4426950408889634

LANES = 128
T_ATT = 256
VMEM_LIMIT = 48 * 1024 * 1024

BF16 = jnp.bfloat16
F32 = jnp.float32


def _rms(xf, g):
    return xf * lax.rsqrt(jnp.mean(xf * xf, axis=-1, keepdims=True) + NORM_EPS) * g


def _nt_dot(a, b):
    return lax.dot_general(a, b, (((1,), (1,)), ((), ())), preferred_element_type=F32)


QK_W = 2 * A_W + 2 * B_W
C_CW = 768
C_PE_GROUP = C_Q_LORA + C_KV_LORA


def _resident(shape):
    return pl.BlockSpec(shape, lambda *_: (0,) * len(shape), pipeline_mode=pl.Buffered(1))


def _proj_kernel(x_ref, g_ref, wqk_ref, cs_ref, wvt_ref, wc_ref, qn_ref, kvn_ref, wuq_ref, wuk_ref,
                 wuvt_ref, qtab_ref, ktab_ref, qk_ref, vta_ref, vtb_ref, qc_ref, kc_ref, vtc_ref):
    h = _rms(x_ref[...], g_ref[...]).astype(BF16)

    qk = (jnp.dot(h, wqk_ref[...], preferred_element_type=F32) * cs_ref[...]).astype(BF16)
    for hd in range(QK_W // LANES):
        qk_ref[hd] = qk[:, hd * LANES:(hd + 1) * LANES]

    vt = _nt_dot(wvt_ref[...], h).astype(BF16)
    vta_ref[...] = vt[:A_W]
    vtb_ref[...] = vt[A_W:]

    c = jnp.dot(h, wc_ref[...], preferred_element_type=F32)
    cqn = _rms(c[:, :C_Q_LORA], qn_ref[...]).astype(BF16)
    ckvn = _rms(c[:, C_Q_LORA:C_PE_GROUP], kvn_ref[...]).astype(BF16)

    q = jnp.dot(cqn, wuq_ref[...], preferred_element_type=F32)
    qtab = qtab_ref[...]
    for hd in range(C_HEADS):
        sl = slice(hd * LANES, (hd + 1) * LANES)
        qc_ref[hd] = (q[:, sl] * qtab).astype(BF16)

    kt = c[:, C_PE_GROUP:] * ktab_ref[...]
    lane = lax.broadcasted_iota(jnp.int32, kt.shape, 1)
    swapped = jnp.where(lane < 96, pltpu.roll(kt, 96, 1), pltpu.roll(kt, 32, 1))
    kp2 = jnp.where(lane >= 64, kt + swapped, 0.0)
    kk = jnp.dot(ckvn, wuk_ref[...], preferred_element_type=F32)
    for hd in range(C_HEADS):
        sl = slice(hd * LANES, (hd + 1) * LANES)
        kc_ref[hd] = (kk[:, sl] + kp2).astype(BF16)

    vtc_ref[...] = _nt_dot(wuvt_ref[...], ckvn).astype(BF16)


def _proj(x2d, g, wqk, colscale, wvt, wc, qn, kvn, wuq, wuk, wuvt, qtab, ktab, batch, tm):
    m = x2d.shape[0]
    per_seq = SEQ // tm
    row = lambda w: pl.BlockSpec((tm, w), lambda i: (i, 0))
    pos = pl.BlockSpec((tm, LANES), lambda i: (i % per_seq, 0))
    vt_spec = lambda w: pl.BlockSpec((None, w, tm), lambda i: (i // per_seq, 0, i % per_seq))
    heads_spec = lambda nh: pl.BlockSpec((nh, tm, LANES), lambda i: (0, i, 0))
    return pl.pallas_call(
        _proj_kernel,
        grid=(m // tm,),
        in_specs=[
            row(D_MODEL),
            _resident((1, D_MODEL)),
            _resident((D_MODEL, QK_W)),
            _resident((1, QK_W)),
            _resident((A_W + B_W, D_MODEL)),
            _resident((D_MODEL, C_CW)),
            _resident((1, C_Q_LORA)),
            _resident((1, C_KV_LORA)),
            _resident((C_Q_LORA, C_HEADS * LANES)),
            _resident((C_KV_LORA, C_HEADS * LANES)),
            _resident((C_OUT_W, C_KV_LORA)),
            pos, pos,
        ],
        out_specs=[heads_spec(QK_W // LANES), vt_spec(A_W), vt_spec(B_W), heads_spec(C_HEADS),
                   heads_spec(C_HEADS), vt_spec(C_OUT_W)],
        out_shape=[
            jax.ShapeDtypeStruct((QK_W // LANES, m, LANES), BF16),
            jax.ShapeDtypeStruct((batch, A_W, SEQ), BF16),
            jax.ShapeDtypeStruct((batch, B_W, SEQ), BF16),
            jax.ShapeDtypeStruct((C_HEADS, m, LANES), BF16),
            jax.ShapeDtypeStruct((C_HEADS, m, LANES), BF16),
            jax.ShapeDtypeStruct((batch, C_OUT_W, SEQ), BF16),
        ],
        compiler_params=pltpu.CompilerParams(
            dimension_semantics=("arbitrary",), vmem_limit_bytes=VMEM_LIMIT),
        name="proj",
    )(x2d, g, wqk, colscale, wvt, wc, qn, kvn, wuq, wuk, wuvt, qtab, ktab)


NQ_ATT = SEQ // T_ATT
KPOS_TERMS = 3
SUM_ROWS = 16


def _key_tile(jt):
    return slice(jt * T_ATT, (jt + 1) * T_ATT)


def _causal_exp2(st, m):
    n, half = st.shape[0], T_ATT // 2
    live = jnp.exp2(st[:n - half] - m).astype(BF16)
    corner = jnp.exp2(st[n - half:, half:] - m[:, half:]).astype(BF16)
    dead = jnp.zeros((half, half), BF16)
    return jnp.concatenate([live, jnp.concatenate([dead, corner], axis=1)], axis=0)


def _attend(qs_of, k_rows, vt_cols, tab_ref, bias_off_diag, finish):
    scored = None
    probs = None
    for t in range(NQ_ATT + 2):
        new_scored = None
        if t < NQ_ATT:
            n = (t + 1) * T_ATT
            new_scored = []
            for s, q in enumerate(qs_of(t)):
                st = _nt_dot(k_rows(s, n), q)
                if bias_off_diag:
                    st = st + tab_ref[tab_ref.shape[0] - n:, :]
                elif t == 0:
                    st = st + tab_ref[...]
                else:
                    st = jnp.concatenate([st[:n - T_ATT], st[n - T_ATT:] + tab_ref[...]], axis=0)
                new_scored.append((st, jnp.max(st, axis=0, keepdims=True)))
        new_probs = None
        if scored is not None:
            new_probs = [_causal_exp2(st, m) for st, m in scored]
        if probs is not None:
            n = (t - 1) * T_ATT
            ones_rows = (lax.broadcasted_iota(jnp.int32, (SUM_ROWS, n), 0) == 0).astype(BF16)
            outs = []
            for s, p in enumerate(probs):
                vt = jnp.concatenate([vt_cols(s, n), ones_rows], axis=0)
                a = jnp.dot(vt, p, preferred_element_type=F32)
                outs.append((a[:-SUM_ROWS], a[-SUM_ROWS:-SUM_ROWS + 1]))
            finish(t - 2, outs)
        scored, probs = new_scored, new_probs


def _diff_attn_kernel(lam_init, q_ref, k_ref, vt_ref, tab_ref, kpos_ref, dl_ref, gain_ref, o_ref):
    dl = dl_ref[...]
    lam = (jnp.exp(jnp.sum(dl[0:1] * dl[1:2], axis=-1, keepdims=True))
           - jnp.exp(jnp.sum(dl[2:3] * dl[3:4], axis=-1, keepdims=True)) + lam_init)
    gain = gain_ref[...] * (1.0 - lam_init)
    lane = lax.broadcasted_iota(jnp.int32, (T_ATT, LANES), 1)
    q_ones = (lane < KPOS_TERMS).astype(BF16)

    def qs_of(qi):
        q = q_ref[_key_tile(qi), :]
        zero = jnp.zeros_like(q)
        return [jnp.concatenate([jnp.where(lane < A_HEAD_DIM, q, zero), q_ones], axis=1),
                jnp.concatenate([jnp.where(lane >= A_HEAD_DIM, q, zero), q_ones], axis=1)]

    def finish(qi, outs):
        (a0, l0), (a1, l1) = outs
        o = a0 / l0 - lam * (a1 / l1)
        y = o * lax.rsqrt(jnp.mean(o * o, axis=0, keepdims=True) + NORM_EPS) * gain
        o_ref[_key_tile(qi), :] = y.T.astype(BF16)

    _attend(qs_of, lambda s, n: jnp.concatenate([k_ref[:n, :], kpos_ref[:n, :]], axis=1),
            lambda s, n: vt_ref[:, :n], tab_ref, False, finish)


def _single_attn_kernel(q_ref, k_ref, vt_ref, tab_ref, o_ref):
    def finish(qi, outs):
        (a, l), = outs
        o_ref[_key_tile(qi), :] = (a / l).T.astype(BF16)

    _attend(lambda qi: [q_ref[_key_tile(qi), :]], lambda s, n: k_ref[:n, :],
            lambda s, n: vt_ref[:, :n], tab_ref, True, finish)


def _pair_attn_kernel(q_ref, k_ref, vt_ref, tab_ref, o_ref):
    def finish(qi, outs):
        (a0, l0), (a1, l1) = outs
        o_ref[_key_tile(qi), :] = jnp.concatenate([a0 / l0, a1 / l1], axis=0).T.astype(BF16)

    _attend(lambda qi: [q_ref[0, _key_tile(qi), :], q_ref[1, _key_tile(qi), :]],
            lambda s, n: k_ref[s, :n, :],
            lambda s, n: vt_ref[s * C_V_DIM:(s + 1) * C_V_DIM, :n],
            tab_ref, False, finish)


def _flash_call(kernel, name, q_arr, q_head0, k_arr, k_head0, vt_arr, tab, extra, extra_specs,
                batch, steps, heads_per_step, tab_per_head):
    m = q_arr.shape[1]
    vrows = vt_arr.shape[1] // steps
    if heads_per_step == 1:
        qk_spec = lambda h0: pl.BlockSpec((None, SEQ, LANES), lambda h, b: (h0 + h, b, 0))
    else:
        qk_spec = lambda h0: pl.BlockSpec((heads_per_step, SEQ, LANES),
                                          lambda h, b: (h0 // heads_per_step + h, b, 0))
    in_specs = [
        qk_spec(q_head0),
        qk_spec(k_head0),
        pl.BlockSpec((None, vrows, SEQ), lambda h, b: (b, h, 0)),
        pl.BlockSpec((None,) + tab.shape[1:],
                     (lambda h, b: (h, 0, 0)) if tab_per_head else (lambda h, b: (0, 0, 0))),
    ] + extra_specs
    return pl.pallas_call(
        kernel,
        grid=(steps, batch),
        in_specs=in_specs,
        out_specs=pl.BlockSpec((None, SEQ, LANES), lambda h, b: (h, b, 0)),
        out_shape=jax.ShapeDtypeStruct((steps, m, LANES), BF16),
        compiler_params=pltpu.CompilerParams(
            dimension_semantics=("arbitrary", "arbitrary"), vmem_limit_bytes=VMEM_LIMIT),
        name=name,
    )(q_arr, k_arr, vt_arr, tab, *extra)


def _mix_kernel(x_ref, g_ref, oa_ref, ob_ref, oc_ref, wg_ref, wa_ref, wb_ref, wc_ref, wo_ref, o_ref):
    x = x_ref[...]
    h = _rms(x, g_ref[...]).astype(BF16)
    mix = None
    for n, (o_br, w_br) in enumerate(((oa_ref, wa_ref), (ob_ref, wb_ref), (oc_ref, wc_ref))):
        gate = jax.nn.sigmoid(jnp.dot(h, wg_ref[:, n * D_MODEL:(n + 1) * D_MODEL], preferred_element_type=F32))
        o = jnp.concatenate([o_br[hd] for hd in range(o_br.shape[0])], axis=1)
        term = gate * jnp.dot(o, w_br[...], preferred_element_type=F32)
        mix = term if mix is None else mix + term
    o_ref[...] = x + jnp.dot(mix.astype(BF16), wo_ref[...], preferred_element_type=F32)


def _mix(x2d, g, oa, ob, oc, wg, wa, wb, wc, wo, tm):
    m = x2d.shape[0]
    row = lambda w: pl.BlockSpec((tm, w), lambda i: (i, 0))
    heads = lambda w: pl.BlockSpec((w // LANES, tm, LANES), lambda i: (0, i, 0))
    return pl.pallas_call(
        _mix_kernel,
        grid=(m // tm,),
        in_specs=[row(D_MODEL), _resident((1, D_MODEL)), heads(A_W), heads(B_W), heads(C_OUT_W),
                  _resident((D_MODEL, 3 * D_MODEL)), _resident((A_W, D_MODEL)), _resident((B_W, D_MODEL)),
                  _resident((C_OUT_W, D_MODEL)), _resident((D_MODEL, D_MODEL))],
        out_specs=row(D_MODEL),
        out_shape=jax.ShapeDtypeStruct((m, D_MODEL), F32),
        compiler_params=pltpu.CompilerParams(
            dimension_semantics=("arbitrary",), vmem_limit_bytes=VMEM_LIMIT),
        name="mix",
    )(x2d, g, oa, ob, oc, wg, wa, wb, wc, wo)


def _ffn_kernel(final, x_ref, g_ref, wg_ref, wu_ref, wd_ref, fg_ref, o_ref):
    half = x_ref.shape[0] // 2
    for part in range(2):
        rows = slice(part * half, (part + 1) * half)
        x = x_ref[rows, :]
        h = _rms(x, g_ref[...]).astype(BF16)
        gate = jnp.dot(h, wg_ref[...], preferred_element_type=F32)
        up = jnp.dot(h, wu_ref[...], preferred_element_type=F32)
        act = (gate * jax.nn.sigmoid(gate) * up).astype(BF16)
        y = x + jnp.dot(act, wd_ref[...], preferred_element_type=F32)
        o_ref[rows, :] = _rms(y, fg_ref[...]) if final else y


def _ffn(x2d, g, wg, wu, wd, fg, final, tm):
    m = x2d.shape[0]
    return pl.pallas_call(
        functools.partial(_ffn_kernel, final),
        grid=(m // tm,),
        in_specs=[
            pl.BlockSpec((tm, D_MODEL), lambda i: (i, 0)),
            _resident((1, D_MODEL)),
            _resident((D_MODEL, FFN_HIDDEN)),
            _resident((D_MODEL, FFN_HIDDEN)),
            _resident((FFN_HIDDEN, D_MODEL)),
            _resident((1, D_MODEL)),
        ],
        out_specs=pl.BlockSpec((tm, D_MODEL), lambda i: (i, 0)),
        out_shape=jax.ShapeDtypeStruct((m, D_MODEL), F32),
        compiler_params=pltpu.CompilerParams(
            dimension_semantics=("arbitrary",), vmem_limit_bytes=VMEM_LIMIT),
        name="ffn",
    )(x2d, g, wg, wu, wd, fg)


def _alibi_slopes():
    n = A_HEADS + B_HEADS
    s = 2.0 ** (-8.0 * np.arange(1, n + 1, dtype=np.float64) / n)
    return s[0::2], s[1::2]


def _bias_tables(slopes, multiplicity, rows):
    r = np.arange(rows, dtype=np.int64)[:, None]
    i = np.arange(T_ATT, dtype=np.int64)[None, :]
    d = (rows - T_ATT) - r + i
    mult = multiplicity(d)
    valid = (d >= 0) & (mult > 0)
    tab = np.log2(np.maximum(mult, 1))[None] - (np.asarray(slopes) * LOG2E)[:, None, None] * d[None]
    return jnp.asarray(np.where(valid[None], tab, NEG_INF).astype(np.float32))


def _kpos_tables(slopes):
    v = (np.asarray(slopes) * LOG2E).astype(np.float32)[:, None] * np.arange(SEQ, dtype=np.float32)[None, :]
    tab = np.zeros(v.shape + (LANES,), BF16)
    for term in range(KPOS_TERMS):
        piece = v.astype(BF16)
        tab[..., term] = piece
        v = v - piece.astype(np.float32)
    return jnp.asarray(tab)


def _dilated_multiplicity(d):
    mult = np.zeros_like(d)
    for window, dil in B_PATTERNS:
        mult = mult + ((d % dil == 0) & (d // dil <= window // dil))
    return mult


def _rope_tables():
    half = C_ROPE_DIM // 2
    inv_freq = ROPE_THETA ** (-np.arange(half, dtype=np.float64) / half)
    ang = np.arange(SEQ, dtype=np.float64)[:, None] * inv_freq[None, :]
    cos = np.concatenate([np.cos(ang)] * 2, axis=-1)
    sin = np.concatenate([np.sin(ang)] * 2, axis=-1)
    scale = (C_NOPE_DIM + C_ROPE_DIM) ** -0.5 * LOG2E
    qtab = np.concatenate([np.full((SEQ, C_NOPE_DIM), scale), cos * scale, sin * scale], axis=-1)
    ktab = np.concatenate([np.zeros((SEQ, C_NOPE_DIM)), cos, sin], axis=-1)
    return jnp.asarray(qtab.astype(np.float32)), jnp.asarray(ktab.astype(np.float32))


def _rotate_half_cols(w):
    half = C_ROPE_DIM // 2
    return jnp.concatenate([-w[..., half:], w[..., :half]], axis=-1)


def _layer_weights(w_in, w_uq, w_ukv):
    sa, sb = A_HEAD_DIM ** -0.5 * LOG2E, B_HEAD_DIM ** -0.5 * LOG2E
    b0 = 3 * A_W
    w_qk = jnp.concatenate([w_in[:, :2 * A_W], w_in[:, b0:b0 + 2 * B_W]], axis=1).astype(BF16)
    w_gate = w_in[:, GATE_OFF:].astype(BF16)
    colscale = jnp.concatenate(
        [jnp.full((A_W,), sa, F32), jnp.ones((A_W,), F32), jnp.full((B_W,), sb, F32),
         jnp.ones((B_W,), F32)])[None, :]
    w_vt = jnp.concatenate([w_in[:, 2 * A_W:3 * A_W], w_in[:, b0 + 2 * B_W:b0 + 3 * B_W]], axis=1).T.astype(BF16)
    w_pe = w_in[:, C_PE_OFF:GATE_OFF]
    w_c = jnp.concatenate(
        [w_in[:, C_Q_OFF:C_PE_OFF], jnp.zeros((D_MODEL, C_NOPE_DIM), F32), w_pe, _rotate_half_cols(w_pe)],
        axis=1).astype(BF16)
    uq = w_uq.reshape(C_Q_LORA, C_HEADS, C_NOPE_DIM + C_ROPE_DIM)
    uq_pe = uq[..., C_NOPE_DIM:]
    w_uq_ext = jnp.concatenate([uq, _rotate_half_cols(uq_pe)], axis=-1).reshape(C_Q_LORA, C_HEADS * LANES).astype(BF16)
    ukv = w_ukv.reshape(C_KV_LORA, C_HEADS, C_NOPE_DIM + C_V_DIM)
    w_uk = jnp.concatenate([ukv[..., :C_NOPE_DIM], jnp.zeros((C_KV_LORA, C_HEADS, LANES - C_NOPE_DIM), F32)],
                           axis=-1).reshape(C_KV_LORA, C_HEADS * LANES).astype(BF16)
    w_uvt = ukv[..., C_NOPE_DIM:].reshape(C_KV_LORA, C_OUT_W).T.astype(BF16)
    return w_qk, colscale, w_gate, w_vt, w_c, w_uq_ext, w_uk, w_uvt


def kernel(x, attn_norm, w_in, diff_lambda, diff_norm, mla_q_norm, mla_w_uq, mla_kv_norm, mla_w_ukv,
           w_branch_a, w_branch_b, w_branch_c, w_out, ffn_norm, w_ffn_gate, w_ffn_up, w_ffn_down,
           final_norm):
    batch, seq, d_model = x.shape
    assert (seq, d_model) == (SEQ, D_MODEL)
    m = batch * seq
    tm_proj, tm_ffn = 1024, 512

    slopes_a, slopes_b = _alibi_slopes()
    kpos_a = _kpos_tables(slopes_a)
    tab_b = _bias_tables(slopes_b, _dilated_multiplicity, SEQ)
    tab_c = _bias_tables(np.zeros((1,)), np.ones_like, T_ATT)
    qtab, ktab = _rope_tables()

    x2d = x.reshape(m, D_MODEL)
    for l in range(DEPTH):
        w_qk, colscale, w_gate, w_vt, w_c, w_uq_ext, w_uk, w_uvt = _layer_weights(
            w_in[l], mla_w_uq[l], mla_w_ukv[l])
        g_attn = attn_norm[l][None, :]
        qk, vta, vtb, qc, kc, vtc = _proj(
            x2d, g_attn, w_qk, colscale, w_vt, w_c, mla_q_norm[l][None, :], mla_kv_norm[l][None, :],
            w_uq_ext, w_uk, w_uvt, qtab, ktab, batch, tm_proj)

        lam_init = 0.8 - 0.6 * math.exp(-0.3 * l)
        oa = _flash_call(
            functools.partial(_diff_attn_kernel, lam_init), "diff_attn", qk, 0, qk, A_HEADS, vta, tab_c,
            [kpos_a, diff_lambda[l], diff_norm[l][:, None]],
            [pl.BlockSpec((None, SEQ, LANES), lambda h, b: (h, 0, 0)),
             pl.BlockSpec((4, A_HEAD_DIM), lambda h, b: (0, 0)),
             pl.BlockSpec((2 * A_HEAD_DIM, 1), lambda h, b: (0, 0))],
            batch, A_HEADS, 1, False)
        ob = _flash_call(_single_attn_kernel, "dilated_attn", qk, 2 * A_HEADS, qk, 2 * A_HEADS + B_HEADS, vtb, tab_b,
                         [], [], batch, B_HEADS, 1, True)
        oc = _flash_call(_pair_attn_kernel, "mla_attn", qc, 0, kc, 0, vtc, tab_c,
                         [], [], batch, C_HEADS // 2, 2, False)

        x2d = _mix(x2d, g_attn, oa, ob, oc, w_gate, w_branch_a[l].astype(BF16), w_branch_b[l].astype(BF16),
                   w_branch_c[l].astype(BF16), w_out[l].astype(BF16), tm_proj)
        x2d = _ffn(x2d, ffn_norm[l][None, :], w_ffn_gate[l].astype(BF16), w_ffn_up[l].astype(BF16),
                   w_ffn_down[l].astype(BF16), final_norm[None, :], l == DEPTH - 1, tm_ffn)
    return x2d.reshape(batch, seq, d_model)
```

```python
import functools
import math

import jax
import jax.numpy as jnp
import numpy as np
from jax import lax
from jax.experimental import pallas as pl
from jax.experimental.pallas import tpu as pltpu

D_MODEL = 1024
SEQ = 2048
DEPTH = 2
A_HEADS = 4
A_HEAD_DIM = 64
B_HEADS = 4
B_HEAD_DIM = 128
B_PATTERNS = ((128, 1), (512, 4), (2048, 16))
C_HEADS = 8
C_NOPE_DIM = 64
C_ROPE_DIM = 32
C_V_DIM = 64
C_Q_LORA = 384
C_KV_LORA = 256
ROPE_THETA = 10000.0
FFN_HIDDEN = 2816
A_W = A_HEADS * 2 * A_HEAD_DIM
B_W = B_HEADS * B_HEAD_DIM
C_OUT_W = C_HEADS * C_V_DIM
C_Q_OFF = 2 * A_W + A_W + 3 * B_W
C_KV_OFF = C_Q_OFF + C_Q_LORA
C_PE_OFF = C_KV_OFF + C_KV_LORA
GATE_OFF = C_PE_OFF + C_ROPE_DIM
NORM_EPS = 1e-6
NEG_INF = -1e30
LOG2E = 1.4426950408889634

LANES = 128
T_ATT = 256
VMEM_LIMIT = 48 * 1024 * 1024

BF16 = jnp.bfloat16
F32 = jnp.float32


def _rms(xf, g):
    return xf * lax.rsqrt(jnp.mean(xf * xf, axis=-1, keepdims=True) + NORM_EPS) * g


def _nt_dot(a, b):
    return lax.dot_general(a, b, (((1,), (1,)), ((), ())), preferred_element_type=F32)


QK_W = 2 * A_W + 2 * B_W
C_CW = 768
C_PE_GROUP = C_Q_LORA + C_KV_LORA


def _resident(shape, layer=None):
    if layer is None:
        return pl.BlockSpec(shape, lambda *_: (0,) * len(shape), pipeline_mode=pl.Buffered(1))
    return pl.BlockSpec((None,) + shape, lambda *_: (layer,) + (0,) * len(shape),
                        pipeline_mode=pl.Buffered(1))


def _proj_kernel(x_ref, g_ref, wqk_ref, cs_ref, wvt_ref, wc_ref, qn_ref, kvn_ref, wuq_ref, wuk_ref,
                 wuvt_ref, qtab_ref, ktab_ref, qk_ref, vta_ref, vtb_ref, qc_ref, kc_ref, vtc_ref):
    h = _rms(x_ref[...], g_ref[...]).astype(BF16)

    qk = (jnp.dot(h, wqk_ref[...], preferred_element_type=F32) * cs_ref[...]).astype(BF16)
    for hd in range(QK_W // LANES):
        qk_ref[hd] = qk[:, hd * LANES:(hd + 1) * LANES]

    vt = _nt_dot(wvt_ref[...], h).astype(BF16)
    vta_ref[...] = vt[:A_W]
    vtb_ref[...] = vt[A_W:]

    c = jnp.dot(h, wc_ref[...], preferred_element_type=F32)
    cqn = _rms(c[:, :C_Q_LORA], qn_ref[...]).astype(BF16)
    ckvn = _rms(c[:, C_Q_LORA:C_PE_GROUP], kvn_ref[...]).astype(BF16)

    q = jnp.dot(cqn, wuq_ref[...], preferred_element_type=F32)
    qtab = qtab_ref[...]
    for hd in range(C_HEADS):
        sl = slice(hd * LANES, (hd + 1) * LANES)
        qc_ref[hd] = (q[:, sl] * qtab).astype(BF16)

    kt = c[:, C_PE_GROUP:] * ktab_ref[...]
    lane = lax.broadcasted_iota(jnp.int32, kt.shape, 1)
    swapped = jnp.where(lane < 96, pltpu.roll(kt, 96, 1), pltpu.roll(kt, 32, 1))
    kp2 = jnp.where(lane >= 64, kt + swapped, 0.0)
    kk = jnp.dot(ckvn, wuk_ref[...], preferred_element_type=F32)
    for hd in range(C_HEADS):
        sl = slice(hd * LANES, (hd + 1) * LANES)
        kc_ref[hd] = (kk[:, sl] + kp2).astype(BF16)

    vtc_ref[...] = _nt_dot(wuvt_ref[...], ckvn).astype(BF16)


def _proj(x2d, g, wqk, colscale, wvt, wc, qn, kvn, wuq, wuk, wuvt, qtab, ktab, layer, batch, tm):
    m = x2d.shape[0]
    per_seq = SEQ // tm
    row = lambda w: pl.BlockSpec((tm, w), lambda i: (i, 0))
    pos = pl.BlockSpec((tm, LANES), lambda i: (i % per_seq, 0))
    vt_spec = lambda w: pl.BlockSpec((None, w, tm), lambda i: (i // per_seq, 0, i % per_seq))
    heads_spec = lambda nh: pl.BlockSpec((nh, tm, LANES), lambda i: (0, i, 0))
    return pl.pallas_call(
        _proj_kernel,
        grid=(m // tm,),
        in_specs=[
            row(D_MODEL),
            _resident((1, D_MODEL)),
            _resident((D_MODEL, QK_W), layer),
            _resident((1, QK_W)),
            _resident((A_W + B_W, D_MODEL), layer),
            _resident((D_MODEL, C_CW), layer),
            _resident((1, C_Q_LORA)),
            _resident((1, C_KV_LORA)),
            _resident((C_Q_LORA, C_HEADS * LANES), layer),
            _resident((C_KV_LORA, C_HEADS * LANES), layer),
            _resident((C_OUT_W, C_KV_LORA), layer),
            pos, pos,
        ],
        out_specs=[heads_spec(QK_W // LANES), vt_spec(A_W), vt_spec(B_W), heads_spec(C_HEADS),
                   heads_spec(C_HEADS), vt_spec(C_OUT_W)],
        out_shape=[
            jax.ShapeDtypeStruct((QK_W // LANES, m, LANES), BF16),
            jax.ShapeDtypeStruct((batch, A_W, SEQ), BF16),
            jax.ShapeDtypeStruct((batch, B_W, SEQ), BF16),
            jax.ShapeDtypeStruct((C_HEADS, m, LANES), BF16),
            jax.ShapeDtypeStruct((C_HEADS, m, LANES), BF16),
            jax.ShapeDtypeStruct((batch, C_OUT_W, SEQ), BF16),
        ],
        compiler_params=pltpu.CompilerParams(
            dimension_semantics=("arbitrary",), vmem_limit_bytes=VMEM_LIMIT),
        name="proj",
    )(x2d, g, wqk, colscale, wvt, wc, qn, kvn, wuq, wuk, wuvt, qtab, ktab)


NQ_ATT = SEQ // T_ATT
KPOS_TERMS = 3
SUM_ROWS = 16


def _key_tile(jt):
    return slice(jt * T_ATT, (jt + 1) * T_ATT)


def _causal_exp2(st, m):
    n, half = st.shape[0], T_ATT // 2
    live = jnp.exp2(st[:n - half] - m).astype(BF16)
    corner = jnp.exp2(st[n - half:, half:] - m[:, half:]).astype(BF16)
    dead = jnp.zeros((half, half), BF16)
    return jnp.concatenate([live, jnp.concatenate([dead, corner], axis=1)], axis=0)


def _attend(qs_of, k_rows, vt_cols, tab_ref, bias_off_diag, finish):
    scored = None
    probs = None
    for t in range(NQ_ATT + 2):
        new_scored = None
        if t < NQ_ATT:
            n = (t + 1) * T_ATT
            new_scored = []
            for s, q in enumerate(qs_of(t)):
                st = _nt_dot(k_rows(s, n), q)
                if bias_off_diag:
                    st = st + tab_ref[tab_ref.shape[0] - n:, :]
                elif t == 0:
                    st = st + tab_ref[...]
                else:
                    st = jnp.concatenate([st[:n - T_ATT], st[n - T_ATT:] + tab_ref[...]], axis=0)
                new_scored.append((st, jnp.max(st, axis=0, keepdims=True)))
        new_probs = None
        if scored is not None:
            new_probs = [_causal_exp2(st, m) for st, m in scored]
        if probs is not None:
            n = (t - 1) * T_ATT
            ones_rows = (lax.broadcasted_iota(jnp.int32, (SUM_ROWS, n), 0) == 0).astype(BF16)
            outs = []
            for s, p in enumerate(probs):
                vt = jnp.concatenate([vt_cols(s, n), ones_rows], axis=0)
                a = jnp.dot(vt, p, preferred_element_type=F32)
                outs.append((a[:-SUM_ROWS], a[-SUM_ROWS:-SUM_ROWS + 1]))
            finish(t - 2, outs)
        scored, probs = new_scored, new_probs


def _diff_attn_kernel(lam_init, q_ref, k_ref, vt_ref, tab_ref, kpos_ref, dl_ref, gain_ref, o_ref):
    dl = dl_ref[...]
    lam = (jnp.exp(jnp.sum(dl[0:1] * dl[1:2], axis=-1, keepdims=True))
           - jnp.exp(jnp.sum(dl[2:3] * dl[3:4], axis=-1, keepdims=True)) + lam_init)
    gain = gain_ref[...] * (1.0 - lam_init)
    lane = lax.broadcasted_iota(jnp.int32, (T_ATT, LANES), 1)
    q_ones = (lane < KPOS_TERMS).astype(BF16)

    def qs_of(qi):
        q = q_ref[_key_tile(qi), :]
        zero = jnp.zeros_like(q)
        return [jnp.concatenate([jnp.where(lane < A_HEAD_DIM, q, zero), q_ones], axis=1),
                jnp.concatenate([jnp.where(lane >= A_HEAD_DIM, q, zero), q_ones], axis=1)]

    def finish(qi, outs):
        (a0, l0), (a1, l1) = outs
        o = a0 / l0 - lam * (a1 / l1)
        y = o * lax.rsqrt(jnp.mean(o * o, axis=0, keepdims=True) + NORM_EPS) * gain
        o_ref[_key_tile(qi), :] = y.T.astype(BF16)

    _attend(qs_of, lambda s, n: jnp.concatenate([k_ref[:n, :], kpos_ref[:n, :]], axis=1),
            lambda s, n: vt_ref[:, :n], tab_ref, False, finish)


def _single_attn_kernel(q_ref, k_ref, vt_ref, tab_ref, o_ref):
    def finish(qi, outs):
        (a, l), = outs
        o_ref[_key_tile(qi), :] = (a / l).T.astype(BF16)

    _attend(lambda qi: [q_ref[_key_tile(qi), :]], lambda s, n: k_ref[:n, :],
            lambda s, n: vt_ref[:, :n], tab_ref, True, finish)


def _pair_attn_kernel(q_ref, k_ref, vt_ref, tab_ref, o_ref):
    def finish(qi, outs):
        (a0, l0), (a1, l1) = outs
        o_ref[_key_tile(qi), :] = jnp.concatenate([a0 / l0, a1 / l1], axis=0).T.astype(BF16)

    _attend(lambda qi: [q_ref[0, _key_tile(qi), :], q_ref[1, _key_tile(qi), :]],
            lambda s, n: k_ref[s, :n, :],
            lambda s, n: vt_ref[s * C_V_DIM:(s + 1) * C_V_DIM, :n],
            tab_ref, False, finish)


def _flash_call(kernel, name, q_arr, q_head0, k_arr, k_head0, vt_arr, tab, extra, extra_specs,
                batch, steps, heads_per_step, tab_per_head):
    m = q_arr.shape[1]
    vrows = vt_arr.shape[1] // steps
    if heads_per_step == 1:
        qk_spec = lambda h0: pl.BlockSpec((None, SEQ, LANES), lambda h, b: (h0 + h, b, 0))
    else:
        qk_spec = lambda h0: pl.BlockSpec((heads_per_step, SEQ, LANES),
                                          lambda h, b: (h0 // heads_per_step + h, b, 0))
    in_specs = [
        qk_spec(q_head0),
        qk_spec(k_head0),
        pl.BlockSpec((None, vrows, SEQ), lambda h, b: (b, h, 0)),
        pl.BlockSpec((None,) + tab.shape[1:],
                     (lambda h, b: (h, 0, 0)) if tab_per_head else (lambda h, b: (0, 0, 0))),
    ] + extra_specs
    return pl.pallas_call(
        kernel,
        grid=(steps, batch),
        in_specs=in_specs,
        out_specs=pl.BlockSpec((None, SEQ, LANES), lambda h, b: (h, b, 0)),
        out_shape=jax.ShapeDtypeStruct((steps, m, LANES), BF16),
        compiler_params=pltpu.CompilerParams(
            dimension_semantics=("arbitrary", "arbitrary"), vmem_limit_bytes=VMEM_LIMIT),
        name=name,
    )(q_arr, k_arr, vt_arr, tab, *extra)


def _mix_kernel(x_ref, g_ref, oa_ref, ob_ref, oc_ref, wg_ref, wa_ref, wb_ref, wc_ref, wo_ref, o_ref):
    x = x_ref[...]
    h = _rms(x, g_ref[...]).astype(BF16)
    mix = None
    for n, (o_br, w_br) in enumerate(((oa_ref, wa_ref), (ob_ref, wb_ref), (oc_ref, wc_ref))):
        gate = jax.nn.sigmoid(jnp.dot(h, wg_ref[:, n * D_MODEL:(n + 1) * D_MODEL], preferred_element_type=F32))
        o = jnp.concatenate([o_br[hd] for hd in range(o_br.shape[0])], axis=1)
        term = gate * jnp.dot(o, w_br[...], preferred_element_type=F32)
        mix = term if mix is None else mix + term
    o_ref[...] = x + jnp.dot(mix.astype(BF16), wo_ref[...], preferred_element_type=F32)


def _mix(x2d, g, oa, ob, oc, wg, wa, wb, wc, wo, layer, tm):
    m = x2d.shape[0]
    row = lambda w: pl.BlockSpec((tm, w), lambda i: (i, 0))
    heads = lambda w: pl.BlockSpec((w // LANES, tm, LANES), lambda i: (0, i, 0))
    return pl.pallas_call(
        _mix_kernel,
        grid=(m // tm,),
        in_specs=[row(D_MODEL), _resident((1, D_MODEL)), heads(A_W), heads(B_W), heads(C_OUT_W),
                  _resident((D_MODEL, 3 * D_MODEL), layer), _resident((A_W, D_MODEL), layer),
                  _resident((B_W, D_MODEL), layer), _resident((C_OUT_W, D_MODEL), layer),
                  _resident((D_MODEL, D_MODEL), layer)],
        out_specs=row(D_MODEL),
        out_shape=jax.ShapeDtypeStruct((m, D_MODEL), F32),
        compiler_params=pltpu.CompilerParams(
            dimension_semantics=("arbitrary",), vmem_limit_bytes=VMEM_LIMIT),
        name="mix",
    )(x2d, g, oa, ob, oc, wg, wa, wb, wc, wo)


def _ffn_kernel(final, x_ref, g_ref, wg_ref, wu_ref, wd_ref, fg_ref, o_ref):
    half = x_ref.shape[0] // 2
    for part in range(2):
        rows = slice(part * half, (part + 1) * half)
        x = x_ref[rows, :]
        h = _rms(x, g_ref[...]).astype(BF16)
        gate = jnp.dot(h, wg_ref[...], preferred_element_type=F32)
        up = jnp.dot(h, wu_ref[...], preferred_element_type=F32)
        act = (gate * jax.nn.sigmoid(gate) * up).astype(BF16)
        y = x + jnp.dot(act, wd_ref[...], preferred_element_type=F32)
        o_ref[rows, :] = _rms(y, fg_ref[...]) if final else y


def _ffn(x2d, g, wg, wu, wd, fg, layer, final, tm):
    m = x2d.shape[0]
    return pl.pallas_call(
        functools.partial(_ffn_kernel, final),
        grid=(m // tm,),
        in_specs=[
            pl.BlockSpec((tm, D_MODEL), lambda i: (i, 0)),
            _resident((1, D_MODEL)),
            _resident((D_MODEL, FFN_HIDDEN), layer),
            _resident((D_MODEL, FFN_HIDDEN), layer),
            _resident((FFN_HIDDEN, D_MODEL), layer),
            _resident((1, D_MODEL)),
        ],
        out_specs=pl.BlockSpec((tm, D_MODEL), lambda i: (i, 0)),
        out_shape=jax.ShapeDtypeStruct((m, D_MODEL), F32),
        compiler_params=pltpu.CompilerParams(
            dimension_semantics=("arbitrary",), vmem_limit_bytes=VMEM_LIMIT),
        name="ffn",
    )(x2d, g, wg, wu, wd, fg)


def _alibi_slopes():
    n = A_HEADS + B_HEADS
    s = 2.0 ** (-8.0 * np.arange(1, n + 1, dtype=np.float64) / n)
    return s[0::2], s[1::2]


def _bias_tables(slopes, multiplicity, rows):
    r = np.arange(rows, dtype=np.int64)[:, None]
    i = np.arange(T_ATT, dtype=np.int64)[None, :]
    d = (rows - T_ATT) - r + i
    mult = multiplicity(d)
    valid = (d >= 0) & (mult > 0)
    tab = np.log2(np.maximum(mult, 1))[None] - (np.asarray(slopes) * LOG2E)[:, None, None] * d[None]
    return jnp.asarray(np.where(valid[None], tab, NEG_INF).astype(np.float32))


def _kpos_tables(slopes):
    v = (np.asarray(slopes) * LOG2E).astype(np.float32)[:, None] * np.arange(SEQ, dtype=np.float32)[None, :]
    tab = np.zeros(v.shape + (LANES,), BF16)
    for term in range(KPOS_TERMS):
        piece = v.astype(BF16)
        tab[..., term] = piece
        v = v - piece.astype(np.float32)
    return jnp.asarray(tab)


def _dilated_multiplicity(d):
    mult = np.zeros_like(d)
    for window, dil in B_PATTERNS:
        mult = mult + ((d % dil == 0) & (d // dil <= window // dil))
    return mult


def _rope_tables():
    half = C_ROPE_DIM // 2
    inv_freq = ROPE_THETA ** (-np.arange(half, dtype=np.float64) / half)
    ang = np.arange(SEQ, dtype=np.float64)[:, None] * inv_freq[None, :]
    cos = np.concatenate([np.cos(ang)] * 2, axis=-1)
    sin = np.concatenate([np.sin(ang)] * 2, axis=-1)
    scale = (C_NOPE_DIM + C_ROPE_DIM) ** -0.5 * LOG2E
    qtab = np.concatenate([np.full((SEQ, C_NOPE_DIM), scale), cos * scale, sin * scale], axis=-1)
    ktab = np.concatenate([np.zeros((SEQ, C_NOPE_DIM)), cos, sin], axis=-1)
    return jnp.asarray(qtab.astype(np.float32)), jnp.asarray(ktab.astype(np.float32))


def _rotate_half_cols(w):
    half = C_ROPE_DIM // 2
    return jnp.concatenate([-w[..., half:], w[..., :half]], axis=-1)


def _stacked_weights(w_in, w_uq, w_ukv):
    depth = w_in.shape[0]
    sa, sb = A_HEAD_DIM ** -0.5 * LOG2E, B_HEAD_DIM ** -0.5 * LOG2E
    b0 = 3 * A_W
    w_qk = jnp.concatenate([w_in[..., :2 * A_W], w_in[..., b0:b0 + 2 * B_W]], axis=-1).astype(BF16)
    w_gate = w_in[..., GATE_OFF:].astype(BF16)
    colscale = jnp.concatenate(
        [jnp.full((A_W,), sa, F32), jnp.ones((A_W,), F32), jnp.full((B_W,), sb, F32),
         jnp.ones((B_W,), F32)])[None, :]
    w_v = jnp.concatenate([w_in[..., 2 * A_W:3 * A_W], w_in[..., b0 + 2 * B_W:b0 + 3 * B_W]], axis=-1)
    w_vt = jnp.swapaxes(w_v, -1, -2).astype(BF16)
    w_pe = w_in[..., C_PE_OFF:GATE_OFF]
    w_c = jnp.concatenate(
        [w_in[..., C_Q_OFF:C_PE_OFF], jnp.zeros((depth, D_MODEL, C_NOPE_DIM), F32), w_pe, _rotate_half_cols(w_pe)],
        axis=-1).astype(BF16)
    uq = w_uq.reshape(depth, C_Q_LORA, C_HEADS, C_NOPE_DIM + C_ROPE_DIM)
    uq_pe = uq[..., C_NOPE_DIM:]
    w_uq_ext = jnp.concatenate([uq, _rotate_half_cols(uq_pe)], axis=-1).reshape(
        depth, C_Q_LORA, C_HEADS * LANES).astype(BF16)
    ukv = w_ukv.reshape(depth, C_KV_LORA, C_HEADS, C_NOPE_DIM + C_V_DIM)
    w_uk = jnp.concatenate(
        [ukv[..., :C_NOPE_DIM], jnp.zeros((depth, C_KV_LORA, C_HEADS, LANES - C_NOPE_DIM), F32)],
        axis=-1).reshape(depth, C_KV_LORA, C_HEADS * LANES).astype(BF16)
    w_uvt = jnp.swapaxes(ukv[..., C_NOPE_DIM:].reshape(depth, C_KV_LORA, C_OUT_W), -1, -2).astype(BF16)
    return w_qk, colscale, w_gate, w_vt, w_c, w_uq_ext, w_uk, w_uvt


def kernel(x, attn_norm, w_in, diff_lambda, diff_norm, mla_q_norm, mla_w_uq, mla_kv_norm, mla_w_ukv,
           w_branch_a, w_branch_b, w_branch_c, w_out, ffn_norm, w_ffn_gate, w_ffn_up, w_ffn_down,
           final_norm):
    batch, seq, d_model = x.shape
    assert (seq, d_model) == (SEQ, D_MODEL)
    m = batch * seq
    tm_proj, tm_ffn = 1024, 512

    slopes_a, slopes_b = _alibi_slopes()
    kpos_a = _kpos_tables(slopes_a)
    tab_b = _bias_tables(slopes_b, _dilated_multiplicity, SEQ)
    tab_c = _bias_tables(np.zeros((1,)), np.ones_like, T_ATT)
    qtab, ktab = _rope_tables()

    wa, wb, wc, wo = (w.astype(BF16) for w in (w_branch_a, w_branch_b, w_branch_c, w_out))
    wfg, wfu, wfd = (w.astype(BF16) for w in (w_ffn_gate, w_ffn_up, w_ffn_down))

    w_qk, colscale, w_gate, w_vt, w_c, w_uq_ext, w_uk, w_uvt = _stacked_weights(w_in, mla_w_uq, mla_w_ukv)

    x2d = x.reshape(m, D_MODEL)
    for l in range(DEPTH):
        g_attn = attn_norm[l][None, :]
        qk, vta, vtb, qc, kc, vtc = _proj(
            x2d, g_attn, w_qk, colscale, w_vt, w_c, mla_q_norm[l][None, :], mla_kv_norm[l][None, :],
            w_uq_ext, w_uk, w_uvt, qtab, ktab, l, batch, tm_proj)

        lam_init = 0.8 - 0.6 * math.exp(-0.3 * l)
        oa = _flash_call(
            functools.partial(_diff_attn_kernel, lam_init), "diff_attn", qk, 0, qk, A_HEADS, vta, tab_c,
            [kpos_a, diff_lambda[l], diff_norm[l][:, None]],
            [pl.BlockSpec((None, SEQ, LANES), lambda h, b: (h, 0, 0)),
             pl.BlockSpec((4, A_HEAD_DIM), lambda h, b: (0, 0)),
             pl.BlockSpec((2 * A_HEAD_DIM, 1), lambda h, b: (0, 0))],
            batch, A_HEADS, 1, False)
        ob = _flash_call(_single_attn_kernel, "dilated_attn", qk, 2 * A_HEADS, qk, 2 * A_HEADS + B_HEADS, vtb, tab_b,
                         [], [], batch, B_HEADS, 1, True)
        oc = _flash_call(_pair_attn_kernel, "mla_attn", qc, 0, kc, 0, vtc, tab_c,
                         [], [], batch, C_HEADS // 2, 2, False)

        x2d = _mix(x2d, g_attn, oa, ob, oc, w_gate, wa, wb, wc, wo, l, tm_proj)
        x2d = _ffn(x2d, ffn_norm[l][None, :], wfg, wfu, wfd, final_norm[None, :], l, l == DEPTH - 1, tm_ffn)
    return x2d.reshape(batch, seq, d_model)
```

```python
import functools
import math

import jax
import jax.numpy as jnp
import numpy as np
from jax import lax
from jax.experimental import pallas as pl
from jax.experimental.pallas import tpu as pltpu

D_MODEL = 1024
SEQ = 2048
DEPTH = 2
A_HEADS = 4
A_HEAD_DIM = 64
B_HEADS = 4
B_HEAD_DIM = 128
B_PATTERNS = ((128, 1), (512, 4), (2048, 16))
C_HEADS = 8
C_NOPE_DIM = 64
C_ROPE_DIM = 32
C_V_DIM = 64
C_Q_LORA = 384
C_KV_LORA = 256
ROPE_THETA = 10000.0
FFN_HIDDEN = 2816
A_W = A_HEADS * 2 * A_HEAD_DIM
B_W = B_HEADS * B_HEAD_DIM
C_OUT_W = C_HEADS * C_V_DIM
C_Q_OFF = 2 * A_W + A_W + 3 * B_W
C_KV_OFF = C_Q_OFF + C_Q_LORA
C_PE_OFF = C_KV_OFF + C_KV_LORA
GATE_OFF = C_PE_OFF + C_ROPE_DIM
NORM_EPS = 1e-6
NEG_INF = -1e30
LOG2E = 1.4426950408889634

LANES = 128
T_ATT = 256
VMEM_LIMIT = 48 * 1024 * 1024

BF16 = jnp.bfloat16
F32 = jnp.float32


def _rms(xf, g):
    return xf * lax.rsqrt(jnp.mean(xf * xf, axis=-1, keepdims=True) + NORM_EPS) * g


def _nt_dot(a, b):
    return lax.dot_general(a, b, (((1,), (1,)), ((), ())), preferred_element_type=F32)


QK_W = 2 * A_W + 2 * B_W
C_PE_GROUP = C_Q_LORA + C_KV_LORA
C_CW = C_PE_GROUP + LANES


def _resident(shape, layer=None):
    if layer is None:
        return pl.BlockSpec(shape, lambda *_: (0,) * len(shape), pipeline_mode=pl.Buffered(1))
    return pl.BlockSpec((None,) + shape, lambda *_: (layer,) + (0,) * len(shape),
                        pipeline_mode=pl.Buffered(1))


def _proj_kernel(x_ref, g_ref, wqk_ref, cs_ref, wvt_ref, wc_ref, qn_ref, kvn_ref, wuq_ref, wuk_ref,
                 wuvt_ref, qtab_ref, ktab_ref, qk_ref, vta_ref, vtb_ref, qc_ref, kc_ref, vtc_ref):
    h = _rms(x_ref[...], g_ref[...]).astype(BF16)

    qk = (jnp.dot(h, wqk_ref[...], preferred_element_type=F32) * cs_ref[...]).astype(BF16)
    for hd in range(QK_W // LANES):
        qk_ref[hd] = qk[:, hd * LANES:(hd + 1) * LANES]

    vt = _nt_dot(wvt_ref[...], h).astype(BF16)
    vta_ref[...] = vt[:A_W]
    vtb_ref[...] = vt[A_W:]

    c = jnp.dot(h, wc_ref[...], preferred_element_type=F32)
    cqn = _rms(c[:, :C_Q_LORA], qn_ref[...]).astype(BF16)
    ckvn = _rms(c[:, C_Q_LORA:C_PE_GROUP], kvn_ref[...]).astype(BF16)

    q = jnp.dot(cqn, wuq_ref[...], preferred_element_type=F32)
    qtab = qtab_ref[...]
    for hd in range(C_HEADS):
        sl = slice(hd * LANES, (hd + 1) * LANES)
        qc_ref[hd] = (q[:, sl] * qtab).astype(BF16)

    kt = c[:, C_PE_GROUP:] * ktab_ref[...]
    lane = lax.broadcasted_iota(jnp.int32, kt.shape, 1)
    swapped = jnp.where(lane < C_NOPE_DIM + C_ROPE_DIM, pltpu.roll(kt, LANES - C_ROPE_DIM, 1),
                        pltpu.roll(kt, C_ROPE_DIM, 1))
    kp2 = jnp.where(lane >= C_NOPE_DIM, kt + swapped, 0.0)
    kk = jnp.dot(ckvn, wuk_ref[...], preferred_element_type=F32)
    for hd in range(C_HEADS):
        sl = slice(hd * LANES, (hd + 1) * LANES)
        kc_ref[hd] = (kk[:, sl] + kp2).astype(BF16)

    vtc_ref[...] = _nt_dot(wuvt_ref[...], ckvn).astype(BF16)


def _proj(x2d, g, wqk, colscale, wvt, wc, qn, kvn, wuq, wuk, wuvt, qtab, ktab, layer, batch, tm):
    m = x2d.shape[0]
    per_seq = SEQ // tm
    row = lambda w: pl.BlockSpec((tm, w), lambda i: (i, 0))
    pos = pl.BlockSpec((tm, LANES), lambda i: (i % per_seq, 0))
    vt_spec = lambda w: pl.BlockSpec((None, w, tm), lambda i: (i // per_seq, 0, i % per_seq))
    heads_spec = lambda nh: pl.BlockSpec((nh, tm, LANES), lambda i: (0, i, 0))
    return pl.pallas_call(
        _proj_kernel,
        grid=(m // tm,),
        in_specs=[
            row(D_MODEL),
            _resident((1, D_MODEL)),
            _resident((D_MODEL, QK_W), layer),
            _resident((1, QK_W)),
            _resident((A_W + B_W, D_MODEL), layer),
            _resident((D_MODEL, C_CW), layer),
            _resident((1, C_Q_LORA)),
            _resident((1, C_KV_LORA)),
            _resident((C_Q_LORA, C_HEADS * LANES), layer),
            _resident((C_KV_LORA, C_HEADS * LANES), layer),
            _resident((C_OUT_W, C_KV_LORA), layer),
            pos, pos,
        ],
        out_specs=[heads_spec(QK_W // LANES), vt_spec(A_W), vt_spec(B_W), heads_spec(C_HEADS),
                   heads_spec(C_HEADS), vt_spec(C_OUT_W)],
        out_shape=[
            jax.ShapeDtypeStruct((QK_W // LANES, m, LANES), BF16),
            jax.ShapeDtypeStruct((batch, A_W, SEQ), BF16),
            jax.ShapeDtypeStruct((batch, B_W, SEQ), BF16),
            jax.ShapeDtypeStruct((C_HEADS, m, LANES), BF16),
            jax.ShapeDtypeStruct((C_HEADS, m, LANES), BF16),
            jax.ShapeDtypeStruct((batch, C_OUT_W, SEQ), BF16),
        ],
        compiler_params=pltpu.CompilerParams(
            dimension_semantics=("arbitrary",), vmem_limit_bytes=VMEM_LIMIT),
        name="proj",
    )(x2d, g, wqk, colscale, wvt, wc, qn, kvn, wuq, wuk, wuvt, qtab, ktab)


NQ_ATT = SEQ // T_ATT
KPOS_TERMS = 3
HEADS_PER_STEP = 2
SUM_ROWS = 16


def _key_tile(jt):
    return slice(jt * T_ATT, (jt + 1) * T_ATT)


def _causal_exp2(st, m):
    n, half = st.shape[0], T_ATT // 2
    live = jnp.exp2(st[:n - half] - m).astype(BF16)
    corner = jnp.exp2(st[n - half:, half:] - m[:, half:]).astype(BF16)
    dead = jnp.zeros((half, half), BF16)
    return jnp.concatenate([live, jnp.concatenate([dead, corner], axis=1)], axis=0)


def _attend(jobs):
    tiles = [(job, t) for job in jobs for t in range(NQ_ATT)]
    scored = None
    probs = None
    for step in range(len(tiles) + 2):
        new_scored = None
        if step < len(tiles):
            job, t = tiles[step]
            qs_of, k_rows, _, tab_ref, bias_off_diag, _ = job
            n = (t + 1) * T_ATT
            outs = []
            for s, q in enumerate(qs_of(t)):
                st = _nt_dot(k_rows(s, n), q)
                if bias_off_diag:
                    st = st + tab_ref[tab_ref.shape[0] - n:, :]
                elif t == 0:
                    st = st + tab_ref[...]
                else:
                    st = jnp.concatenate([st[:n - T_ATT], st[n - T_ATT:] + tab_ref[...]], axis=0)
                outs.append((st, jnp.max(st, axis=0, keepdims=True)))
            new_scored = (job, t, outs)
        new_probs = None
        if scored is not None:
            new_probs = scored[:2] + ([_causal_exp2(st, m) for st, m in scored[2]],)
        if probs is not None:
            job, t, ps = probs
            _, _, vt_cols, _, _, finish = job
            n = (t + 1) * T_ATT
            ones_rows = (lax.broadcasted_iota(jnp.int32, (SUM_ROWS, n), 0) == 0).astype(BF16)
            outs = []
            for s, p in enumerate(ps):
                vt = jnp.concatenate([vt_cols(s, n), ones_rows], axis=0)
                a = jnp.dot(vt, p, preferred_element_type=F32)
                outs.append((a[:-SUM_ROWS], a[-SUM_ROWS:-SUM_ROWS + 1]))
            finish(t, outs)
        scored, probs = new_scored, new_probs


def _diff_attn_kernel(lam_init, q_ref, k_ref, vt_ref, tab_ref, kpos_ref, dl_ref, gain_ref, o_ref):
    dl = dl_ref[...]
    lam = (jnp.exp(jnp.sum(dl[0:1] * dl[1:2], axis=-1, keepdims=True))
           - jnp.exp(jnp.sum(dl[2:3] * dl[3:4], axis=-1, keepdims=True)) + lam_init)
    gain = gain_ref[...] * (1.0 - lam_init)
    lane = lax.broadcasted_iota(jnp.int32, (T_ATT, LANES), 1)
    q_ones = (lane < KPOS_TERMS).astype(BF16)
    dv = 2 * A_HEAD_DIM

    def job(g):
        def qs_of(qi):
            q = q_ref[g, _key_tile(qi), :]
            zero = jnp.zeros_like(q)
            return [jnp.concatenate([jnp.where(lane < A_HEAD_DIM, q, zero), q_ones], axis=1),
                    jnp.concatenate([jnp.where(lane >= A_HEAD_DIM, q, zero), q_ones], axis=1)]

        def finish(qi, outs):
            (a0, l0), (a1, l1) = outs
            o = a0 / l0 - lam * (a1 / l1)
            y = o * lax.rsqrt(jnp.mean(o * o, axis=0, keepdims=True) + NORM_EPS) * gain
            o_ref[g, _key_tile(qi), :] = y.T.astype(BF16)

        return (qs_of, lambda s, n: jnp.concatenate([k_ref[g, :n, :], kpos_ref[g, :n, :]], axis=1),
                lambda s, n: vt_ref[g * dv:(g + 1) * dv, :n], tab_ref, False, finish)

    _attend([job(g) for g in range(q_ref.shape[0])])


def _single_attn_kernel(q_ref, k_ref, vt_ref, tab_ref, o_ref):
    def job(g):
        def finish(qi, outs):
            (a, l), = outs
            o_ref[g, _key_tile(qi), :] = (a / l).T.astype(BF16)

        return (lambda qi: [q_ref[g, _key_tile(qi), :]], lambda s, n: k_ref[g, :n, :],
                lambda s, n: vt_ref[g * B_HEAD_DIM:(g + 1) * B_HEAD_DIM, :n], tab_ref.at[g], True, finish)

    _attend([job(g) for g in range(q_ref.shape[0])])


def _pair_attn_kernel(q_ref, k_ref, vt_ref, tab_ref, o_ref):
    def job(g):
        def finish(qi, outs):
            (a0, l0), (a1, l1) = outs
            o_ref[g, _key_tile(qi), :] = jnp.concatenate([a0 / l0, a1 / l1], axis=0).T.astype(BF16)

        return (lambda qi: [q_ref[2 * g, _key_tile(qi), :], q_ref[2 * g + 1, _key_tile(qi), :]],
                lambda s, n: k_ref[2 * g + s, :n, :],
                lambda s, n: vt_ref[(2 * g + s) * C_V_DIM:(2 * g + s + 1) * C_V_DIM, :n],
                tab_ref, False, finish)

    _attend([job(g) for g in range(o_ref.shape[0])])


def _flash_call(kernel, name, q_arr, q_head0, k_arr, k_head0, vt_arr, tab, extra, extra_specs,
                batch, steps, heads_per_step, outs_per_step, tab_per_head):
    m = q_arr.shape[1]
    vrows = vt_arr.shape[1] // steps
    qk_spec = lambda h0: pl.BlockSpec((heads_per_step, SEQ, LANES),
                                      lambda h, b: (h0 // heads_per_step + h, b, 0))
    if tab_per_head:
        tab_spec = pl.BlockSpec((heads_per_step,) + tab.shape[1:], lambda h, b: (h, 0, 0))
    else:
        tab_spec = pl.BlockSpec((None,) + tab.shape[1:], lambda h, b: (0, 0, 0))
    in_specs = [
        qk_spec(q_head0),
        qk_spec(k_head0),
        pl.BlockSpec((None, vrows, SEQ), lambda h, b: (b, h, 0)),
        tab_spec,
    ] + extra_specs
    return pl.pallas_call(
        kernel,
        grid=(steps, batch),
        in_specs=in_specs,
        out_specs=pl.BlockSpec((outs_per_step, SEQ, LANES), lambda h, b: (h, b, 0)),
        out_shape=jax.ShapeDtypeStruct((steps * outs_per_step, m, LANES), BF16),
        compiler_params=pltpu.CompilerParams(
            dimension_semantics=("arbitrary", "arbitrary"), vmem_limit_bytes=VMEM_LIMIT),
        name=name,
    )(q_arr, k_arr, vt_arr, tab, *extra)


def _mix_kernel(x_ref, g_ref, oa_ref, ob_ref, oc_ref, wg_ref, wa_ref, wb_ref, wc_ref, wo_ref, o_ref):
    x = x_ref[...]
    h = _rms(x, g_ref[...]).astype(BF16)
    mix = None
    for n, (o_br, w_br) in enumerate(((oa_ref, wa_ref), (ob_ref, wb_ref), (oc_ref, wc_ref))):
        gate = jax.nn.sigmoid(jnp.dot(h, wg_ref[:, n * D_MODEL:(n + 1) * D_MODEL], preferred_element_type=F32))
        o = jnp.concatenate([o_br[hd] for hd in range(o_br.shape[0])], axis=1)
        term = gate * jnp.dot(o, w_br[...], preferred_element_type=F32)
        mix = term if mix is None else mix + term
    o_ref[...] = x + jnp.dot(mix.astype(BF16), wo_ref[...], preferred_element_type=F32)


def _mix(x2d, g, oa, ob, oc, wg, wa, wb, wc, wo, layer, tm):
    m = x2d.shape[0]
    row = lambda w: pl.BlockSpec((tm, w), lambda i: (i, 0))
    heads = lambda w: pl.BlockSpec((w // LANES, tm, LANES), lambda i: (0, i, 0))
    return pl.pallas_call(
        _mix_kernel,
        grid=(m // tm,),
        in_specs=[row(D_MODEL), _resident((1, D_MODEL)), heads(A_W), heads(B_W), heads(C_OUT_W),
                  _resident((D_MODEL, 3 * D_MODEL), layer), _resident((A_W, D_MODEL), layer),
                  _resident((B_W, D_MODEL), layer), _resident((C_OUT_W, D_MODEL), layer),
                  _resident((D_MODEL, D_MODEL), layer)],
        out_specs=row(D_MODEL),
        out_shape=jax.ShapeDtypeStruct((m, D_MODEL), F32),
        compiler_params=pltpu.CompilerParams(
            dimension_semantics=("arbitrary",), vmem_limit_bytes=VMEM_LIMIT),
        name="mix",
    )(x2d, g, oa, ob, oc, wg, wa, wb, wc, wo)


def _ffn_kernel(final, x_ref, g_ref, wg_ref, wu_ref, wd_ref, fg_ref, o_ref):
    half = x_ref.shape[0] // 2
    for part in range(2):
        rows = slice(part * half, (part + 1) * half)
        x = x_ref[rows, :]
        h = _rms(x, g_ref[...]).astype(BF16)
        gate = jnp.dot(h, wg_ref[...], preferred_element_type=F32)
        up = jnp.dot(h, wu_ref[...], preferred_element_type=F32)
        act = (gate * jax.nn.sigmoid(gate) * up).astype(BF16)
        y = x + jnp.dot(act, wd_ref[...], preferred_element_type=F32)
        o_ref[rows, :] = _rms(y, fg_ref[...]) if final else y


def _ffn(x2d, g, wg, wu, wd, fg, layer, final, tm):
    m = x2d.shape[0]
    return pl.pallas_call(
        functools.partial(_ffn_kernel, final),
        grid=(m // tm,),
        in_specs=[
            pl.BlockSpec((tm, D_MODEL), lambda i: (i, 0)),
            _resident((1, D_MODEL)),
            _resident((D_MODEL, FFN_HIDDEN), layer),
            _resident((D_MODEL, FFN_HIDDEN), layer),
            _resident((FFN_HIDDEN, D_MODEL), layer),
            _resident((1, D_MODEL)),
        ],
        out_specs=pl.BlockSpec((tm, D_MODEL), lambda i: (i, 0)),
        out_shape=jax.ShapeDtypeStruct((m, D_MODEL), F32),
        compiler_params=pltpu.CompilerParams(
            dimension_semantics=("arbitrary",), vmem_limit_bytes=VMEM_LIMIT),
        name="ffn",
    )(x2d, g, wg, wu, wd, fg)


def _alibi_slopes():
    n = A_HEADS + B_HEADS
    s = 2.0 ** (-8.0 * np.arange(1, n + 1, dtype=np.float64) / n)
    return s[0::2], s[1::2]


def _bias_tables(slopes, multiplicity, rows):
    r = np.arange(rows, dtype=np.int64)[:, None]
    i = np.arange(T_ATT, dtype=np.int64)[None, :]
    d = (rows - T_ATT) - r + i
    mult = multiplicity(d)
    valid = (d >= 0) & (mult > 0)
    tab = np.log2(np.maximum(mult, 1))[None] - (np.asarray(slopes) * LOG2E)[:, None, None] * d[None]
    return jnp.asarray(np.where(valid[None], tab, NEG_INF).astype(np.float32))


def _kpos_tables(slopes):
    v = (np.asarray(slopes) * LOG2E).astype(np.float32)[:, None] * np.arange(SEQ, dtype=np.float32)[None, :]
    tab = np.zeros(v.shape + (LANES,), BF16)
    for term in range(KPOS_TERMS):
        piece = v.astype(BF16)
        tab[..., term] = piece
        v = v - piece.astype(np.float32)
    return jnp.asarray(tab)


def _dilated_multiplicity(d):
    mult = np.zeros_like(d)
    for window, dil in B_PATTERNS:
        mult = mult + ((d % dil == 0) & (d // dil <= window // dil))
    return mult


def _rope_tables():
    half = C_ROPE_DIM // 2
    inv_freq = ROPE_THETA ** (-np.arange(half, dtype=np.float64) / half)
    ang = np.arange(SEQ, dtype=np.float64)[:, None] * inv_freq[None, :]
    cos = np.concatenate([np.cos(ang)] * 2, axis=-1)
    sin = np.concatenate([np.sin(ang)] * 2, axis=-1)
    scale = (C_NOPE_DIM + C_ROPE_DIM) ** -0.5 * LOG2E
    qtab = np.concatenate([np.full((SEQ, C_NOPE_DIM), scale), cos * scale, sin * scale], axis=-1)
    ktab = np.concatenate([np.zeros((SEQ, C_NOPE_DIM)), cos, sin], axis=-1)
    return jnp.asarray(qtab.astype(np.float32)), jnp.asarray(ktab.astype(np.float32))


def _rotate_half_cols(w):
    half = C_ROPE_DIM // 2
    return jnp.concatenate([-w[..., half:], w[..., :half]], axis=-1)


def _stacked_weights(w_in, w_uq, w_ukv):
    depth = w_in.shape[0]
    sa, sb = A_HEAD_DIM ** -0.5 * LOG2E, B_HEAD_DIM ** -0.5 * LOG2E
    b0 = 3 * A_W
    w_qk = jnp.concatenate([w_in[..., :2 * A_W], w_in[..., b0:b0 + 2 * B_W]], axis=-1).astype(BF16)
    w_gate = w_in[..., GATE_OFF:].astype(BF16)
    colscale = jnp.concatenate(
        [jnp.full((A_W,), sa, F32), jnp.ones((A_W,), F32), jnp.full((B_W,), sb, F32),
         jnp.ones((B_W,), F32)])[None, :]
    w_v = jnp.concatenate([w_in[..., 2 * A_W:3 * A_W], w_in[..., b0 + 2 * B_W:b0 + 3 * B_W]], axis=-1)
    w_vt = jnp.swapaxes(w_v, -1, -2).astype(BF16)
    w_pe = w_in[..., C_PE_OFF:GATE_OFF]
    w_c = jnp.concatenate(
        [w_in[..., C_Q_OFF:C_PE_OFF], jnp.zeros((depth, D_MODEL, C_NOPE_DIM), F32), w_pe, _rotate_half_cols(w_pe)],
        axis=-1).astype(BF16)
    uq = w_uq.reshape(depth, C_Q_LORA, C_HEADS, C_NOPE_DIM + C_ROPE_DIM)
    uq_pe = uq[..., C_NOPE_DIM:]
    w_uq_ext = jnp.concatenate([uq, _rotate_half_cols(uq_pe)], axis=-1).reshape(
        depth, C_Q_LORA, C_HEADS * LANES).astype(BF16)
    ukv = w_ukv.reshape(depth, C_KV_LORA, C_HEADS, C_NOPE_DIM + C_V_DIM)
    w_uk = jnp.concatenate(
        [ukv[..., :C_NOPE_DIM], jnp.zeros((depth, C_KV_LORA, C_HEADS, LANES - C_NOPE_DIM), F32)],
        axis=-1).reshape(depth, C_KV_LORA, C_HEADS * LANES).astype(BF16)
    w_uvt = jnp.swapaxes(ukv[..., C_NOPE_DIM:].reshape(depth, C_KV_LORA, C_OUT_W), -1, -2).astype(BF16)
    return w_qk, colscale, w_gate, w_vt, w_c, w_uq_ext, w_uk, w_uvt


def kernel(x, attn_norm, w_in, diff_lambda, diff_norm, mla_q_norm, mla_w_uq, mla_kv_norm, mla_w_ukv,
           w_branch_a, w_branch_b, w_branch_c, w_out, ffn_norm, w_ffn_gate, w_ffn_up, w_ffn_down,
           final_norm):
    batch, seq, d_model = x.shape
    assert (seq, d_model) == (SEQ, D_MODEL)
    m = batch * seq
    tm_proj, tm_ffn = 1024, 512

    slopes_a, slopes_b = _alibi_slopes()
    kpos_a = _kpos_tables(slopes_a)
    tab_b = _bias_tables(slopes_b, _dilated_multiplicity, SEQ)
    tab_c = _bias_tables(np.zeros((1,)), np.ones_like, T_ATT)
    qtab, ktab = _rope_tables()

    wa, wb, wc, wo = (w.astype(BF16) for w in (w_branch_a, w_branch_b, w_branch_c, w_out))
    wfg, wfu, wfd = (w.astype(BF16) for w in (w_ffn_gate, w_ffn_up, w_ffn_down))

    w_qk, colscale, w_gate, w_vt, w_c, w_uq_ext, w_uk, w_uvt = _stacked_weights(w_in, mla_w_uq, mla_w_ukv)

    x2d = x.reshape(m, D_MODEL)
    for l in range(DEPTH):
        g_attn = attn_norm[l][None, :]
        qk, vta, vtb, qc, kc, vtc = _proj(
            x2d, g_attn, w_qk, colscale, w_vt, w_c, mla_q_norm[l][None, :], mla_kv_norm[l][None, :],
            w_uq_ext, w_uk, w_uvt, qtab, ktab, l, batch, tm_proj)

        lam_init = 0.8 - 0.6 * math.exp(-0.3 * l)
        oa = _flash_call(
            functools.partial(_diff_attn_kernel, lam_init), "diff_attn", qk, 0, qk, A_HEADS, vta, tab_c,
            [kpos_a, diff_lambda[l], diff_norm[l][:, None]],
            [pl.BlockSpec((HEADS_PER_STEP, SEQ, LANES), lambda h, b: (h, 0, 0)),
             pl.BlockSpec((4, A_HEAD_DIM), lambda h, b: (0, 0)),
             pl.BlockSpec((2 * A_HEAD_DIM, 1), lambda h, b: (0, 0))],
            batch, A_HEADS // HEADS_PER_STEP, HEADS_PER_STEP, HEADS_PER_STEP, False)
        ob = _flash_call(_single_attn_kernel, "dilated_attn", qk, 2 * A_HEADS, qk, 2 * A_HEADS + B_HEADS, vtb, tab_b,
                         [], [], batch, B_HEADS // HEADS_PER_STEP, HEADS_PER_STEP, HEADS_PER_STEP, True)
        oc = _flash_call(_pair_attn_kernel, "mla_attn", qc, 0, kc, 0, vtc, tab_c,
                         [], [], batch, C_HEADS // (2 * HEADS_PER_STEP), 2 * HEADS_PER_STEP, HEADS_PER_STEP, False)

        x2d = _mix(x2d, g_attn, oa, ob, oc, w_gate, wa, wb, wc, wo, l, tm_proj)
        x2d = _ffn(x2d, ffn_norm[l][None, :], wfg, wfu, wfd, final_norm[None, :], l, l == DEPTH - 1, tm_ffn)
    return x2d.reshape(batch, seq, d_model)
```

```python
import functools
import math

import jax
import jax.numpy as jnp
import numpy as np
from jax import lax
from jax.experimental import pallas as pl
from jax.experimental.pallas import tpu as pltpu

D_MODEL = 1024
SEQ = 2048
DEPTH = 2
A_HEADS = 4
A_HEAD_DIM = 64
B_HEADS = 4
B_HEAD_DIM = 128
B_PATTERNS = ((128, 1), (512, 4), (2048, 16))
C_HEADS = 8
C_NOPE_DIM = 64
C_ROPE_DIM = 32
C_V_DIM = 64
C_Q_LORA = 384
C_KV_LORA = 256
ROPE_THETA = 10000.0
FFN_HIDDEN = 2816
A_W = A_HEADS * 2 * A_HEAD_DIM
B_W = B_HEADS * B_HEAD_DIM
C_OUT_W = C_HEADS * C_V_DIM
C_Q_OFF = 2 * A_W + A_W + 3 * B_W
C_KV_OFF = C_Q_OFF + C_Q_LORA
C_PE_OFF = C_KV_OFF + C_KV_LORA
GATE_OFF = C_PE_OFF + C_ROPE_DIM
NORM_EPS = 1e-6
NEG_INF = -1e30
LOG2E = 1.4426950408889634

LANES = 128
T_ATT = 256
VMEM_LIMIT = 48 * 1024 * 1024

BF16 = jnp.bfloat16
F32 = jnp.float32


def _rms(xf, g):
    return xf * lax.rsqrt(jnp.mean(xf * xf, axis=-1, keepdims=True) + NORM_EPS) * g


def _nt_dot(a, b):
    return lax.dot_general(a, b, (((1,), (1,)), ((), ())), preferred_element_type=F32)


def _tn_dot(a, b):
    return lax.dot_general(a, b, (((0,), (1,)), ((), ())), preferred_element_type=F32)


QK_W = 2 * A_W + 2 * B_W
C_PE_GROUP = C_Q_LORA + C_KV_LORA
C_CW = C_PE_GROUP + LANES


def _resident(shape, layer=None):
    if layer is None:
        return pl.BlockSpec(shape, lambda *_: (0,) * len(shape), pipeline_mode=pl.Buffered(1))
    return pl.BlockSpec((None,) + shape, lambda *_: (layer,) + (0,) * len(shape),
                        pipeline_mode=pl.Buffered(1))


def _proj_kernel(x_ref, g_ref, wqk_ref, cs_ref, wv_ref, wc_ref, qn_ref, kvn_ref, wuq_ref, wuk_ref,
                 wuv_ref, qtab_ref, ktab_ref, qk_ref, vta_ref, vtb_ref, qc_ref, kc_ref, vtc_ref):
    h = _rms(x_ref[...], g_ref[...]).astype(BF16)

    qk = (jnp.dot(h, wqk_ref[...], preferred_element_type=F32) * cs_ref[...]).astype(BF16)
    for hd in range(QK_W // LANES):
        qk_ref[hd] = qk[:, hd * LANES:(hd + 1) * LANES]

    vt = _tn_dot(wv_ref[...], h).astype(BF16)
    vta_ref[...] = vt[:A_W]
    vtb_ref[...] = vt[A_W:]

    c = jnp.dot(h, wc_ref[...], preferred_element_type=F32)
    cqn = _rms(c[:, :C_Q_LORA], qn_ref[...]).astype(BF16)
    ckvn = _rms(c[:, C_Q_LORA:C_PE_GROUP], kvn_ref[...]).astype(BF16)

    q = jnp.dot(cqn, wuq_ref[...], preferred_element_type=F32)
    qtab = qtab_ref[...]
    for hd in range(C_HEADS):
        sl = slice(hd * LANES, (hd + 1) * LANES)
        qc_ref[hd] = (q[:, sl] * qtab).astype(BF16)

    kt = c[:, C_PE_GROUP:] * ktab_ref[...]
    lane = lax.broadcasted_iota(jnp.int32, kt.shape, 1)
    swapped = jnp.where(lane < C_NOPE_DIM + C_ROPE_DIM, pltpu.roll(kt, LANES - C_ROPE_DIM, 1),
                        pltpu.roll(kt, C_ROPE_DIM, 1))
    kp2 = jnp.where(lane >= C_NOPE_DIM, kt + swapped, 0.0)
    kk = jnp.dot(ckvn, wuk_ref[...], preferred_element_type=F32)
    for hd in range(C_HEADS):
        sl = slice(hd * LANES, (hd + 1) * LANES)
        kc_ref[hd] = (kk[:, sl] + kp2).astype(BF16)

    vtc_ref[...] = _tn_dot(wuv_ref[...], ckvn).astype(BF16)


def _proj(x2d, g, wqk, colscale, wv, wc, qn, kvn, wuq, wuk, wuv, qtab, ktab, layer, batch, tm):
    m = x2d.shape[0]
    per_seq = SEQ // tm
    row = lambda w: pl.BlockSpec((tm, w), lambda i: (i, 0))
    pos = pl.BlockSpec((tm, LANES), lambda i: (i % per_seq, 0))
    vt_spec = lambda w: pl.BlockSpec((None, w, tm), lambda i: (i // per_seq, 0, i % per_seq))
    heads_spec = lambda nh: pl.BlockSpec((nh, tm, LANES), lambda i: (0, i, 0))
    return pl.pallas_call(
        _proj_kernel,
        grid=(m // tm,),
        in_specs=[
            row(D_MODEL),
            _resident((1, D_MODEL)),
            _resident((D_MODEL, QK_W), layer),
            _resident((1, QK_W)),
            _resident((D_MODEL, A_W + B_W), layer),
            _resident((D_MODEL, C_CW), layer),
            _resident((1, C_Q_LORA)),
            _resident((1, C_KV_LORA)),
            _resident((C_Q_LORA, C_HEADS * LANES), layer),
            _resident((C_KV_LORA, C_HEADS * LANES), layer),
            _resident((C_KV_LORA, C_OUT_W), layer),
            pos, pos,
        ],
        out_specs=[heads_spec(QK_W // LANES), vt_spec(A_W), vt_spec(B_W), heads_spec(C_HEADS),
                   heads_spec(C_HEADS), vt_spec(C_OUT_W)],
        out_shape=[
            jax.ShapeDtypeStruct((QK_W // LANES, m, LANES), BF16),
            jax.ShapeDtypeStruct((batch, A_W, SEQ), BF16),
            jax.ShapeDtypeStruct((batch, B_W, SEQ), BF16),
            jax.ShapeDtypeStruct((C_HEADS, m, LANES), BF16),
            jax.ShapeDtypeStruct((C_HEADS, m, LANES), BF16),
            jax.ShapeDtypeStruct((batch, C_OUT_W, SEQ), BF16),
        ],
        compiler_params=pltpu.CompilerParams(
            dimension_semantics=("arbitrary",), vmem_limit_bytes=VMEM_LIMIT),
        name="proj",
    )(x2d, g, wqk, colscale, wv, wc, qn, kvn, wuq, wuk, wuv, qtab, ktab)


NQ_ATT = SEQ // T_ATT
KPOS_TERMS = 3
HEADS_PER_STEP = 4
SUM_ROWS = 16


def _key_tile(jt):
    return slice(jt * T_ATT, (jt + 1) * T_ATT)


def _causal_exp2(st, m):
    n, half = st.shape[0], T_ATT // 2
    live = jnp.exp2(st[:n - half] - m).astype(BF16)
    corner = jnp.exp2(st[n - half:, half:] - m[:, half:]).astype(BF16)
    dead = jnp.zeros((half, half), BF16)
    return jnp.concatenate([live, jnp.concatenate([dead, corner], axis=1)], axis=0)


def _attend(jobs):
    tiles = [(job, t) for job in jobs for t in range(NQ_ATT)]
    scored = None
    probs = None
    for step in range(len(tiles) + 2):
        new_scored = None
        if step < len(tiles):
            job, t = tiles[step]
            qs_of, k_rows, _, tab_ref, bias_off_diag, _ = job
            n = (t + 1) * T_ATT
            outs = []
            for s, q in enumerate(qs_of(t)):
                st = _nt_dot(k_rows(s, n), q)
                if bias_off_diag:
                    st = st + tab_ref[tab_ref.shape[0] - n:, :]
                elif t == 0:
                    st = st + tab_ref[...]
                else:
                    st = jnp.concatenate([st[:n - T_ATT], st[n - T_ATT:] + tab_ref[...]], axis=0)
                outs.append((st, jnp.max(st, axis=0, keepdims=True)))
            new_scored = (job, t, outs)
        new_probs = None
        if scored is not None:
            new_probs = scored[:2] + ([_causal_exp2(st, m) for st, m in scored[2]],)
        if probs is not None:
            job, t, ps = probs
            _, _, vt_cols, _, _, finish = job
            n = (t + 1) * T_ATT
            ones_rows = (lax.broadcasted_iota(jnp.int32, (SUM_ROWS, n), 0) == 0).astype(BF16)
            outs = []
            for s, p in enumerate(ps):
                vt = jnp.concatenate([vt_cols(s, n), ones_rows], axis=0)
                a = jnp.dot(vt, p, preferred_element_type=F32)
                outs.append((a[:-SUM_ROWS], a[-SUM_ROWS:-SUM_ROWS + 1]))
            finish(t, outs)
        scored, probs = new_scored, new_probs


def _diff_attn_kernel(lam_init, q_ref, k_ref, vt_ref, tab_ref, kpos_ref, dl_ref, gain_ref, o_ref):
    dl = dl_ref[...]
    lam = (jnp.exp(jnp.sum(dl[0:1] * dl[1:2], axis=-1, keepdims=True))
           - jnp.exp(jnp.sum(dl[2:3] * dl[3:4], axis=-1, keepdims=True)) + lam_init)
    gain = gain_ref[...] * (1.0 - lam_init)
    lane = lax.broadcasted_iota(jnp.int32, (T_ATT, LANES), 1)
    q_ones = (lane < KPOS_TERMS).astype(BF16)
    dv = 2 * A_HEAD_DIM

    def job(g):
        def qs_of(qi):
            q = q_ref[g, _key_tile(qi), :]
            zero = jnp.zeros_like(q)
            return [jnp.concatenate([jnp.where(lane < A_HEAD_DIM, q, zero), q_ones], axis=1),
                    jnp.concatenate([jnp.where(lane >= A_HEAD_DIM, q, zero), q_ones], axis=1)]

        def finish(qi, outs):
            (a0, l0), (a1, l1) = outs
            o = a0 / l0 - lam * (a1 / l1)
            y = o * lax.rsqrt(jnp.mean(o * o, axis=0, keepdims=True) + NORM_EPS) * gain
            o_ref[g, _key_tile(qi), :] = y.T.astype(BF16)

        return (qs_of, lambda s, n: jnp.concatenate([k_ref[g, :n, :], kpos_ref[g, :n, :]], axis=1),
                lambda s, n: vt_ref[g * dv:(g + 1) * dv, :n], tab_ref, False, finish)

    _attend([job(g) for g in range(q_ref.shape[0])])


def _single_attn_kernel(q_ref, k_ref, vt_ref, tab_ref, o_ref):
    def job(g):
        def finish(qi, outs):
            (a, l), = outs
            o_ref[g, _key_tile(qi), :] = (a / l).T.astype(BF16)

        return (lambda qi: [q_ref[g, _key_tile(qi), :]], lambda s, n: k_ref[g, :n, :],
                lambda s, n: vt_ref[g * B_HEAD_DIM:(g + 1) * B_HEAD_DIM, :n], tab_ref.at[g], True, finish)

    _attend([job(g) for g in range(q_ref.shape[0])])


def _pair_attn_kernel(q_ref, k_ref, vt_ref, tab_ref, o_ref):
    def job(g):
        def finish(qi, outs):
            (a0, l0), (a1, l1) = outs
            o_ref[g, _key_tile(qi), :] = jnp.concatenate([a0 / l0, a1 / l1], axis=0).T.astype(BF16)

        return (lambda qi: [q_ref[2 * g, _key_tile(qi), :], q_ref[2 * g + 1, _key_tile(qi), :]],
                lambda s, n: k_ref[2 * g + s, :n, :],
                lambda s, n: vt_ref[(2 * g + s) * C_V_DIM:(2 * g + s + 1) * C_V_DIM, :n],
                tab_ref, False, finish)

    _attend([job(g) for g in range(o_ref.shape[0])])


def _flash_call(kernel, name, q_arr, q_head0, k_arr, k_head0, vt_arr, tab, extra, extra_specs,
                batch, steps, heads_per_step, outs_per_step, tab_per_head):
    m = q_arr.shape[1]
    vrows = vt_arr.shape[1] // steps
    qk_spec = lambda h0: pl.BlockSpec((heads_per_step, SEQ, LANES),
                                      lambda h, b: (h0 // heads_per_step + h, b, 0))
    if tab_per_head:
        tab_spec = pl.BlockSpec((heads_per_step,) + tab.shape[1:], lambda h, b: (h, 0, 0))
    else:
        tab_spec = pl.BlockSpec((None,) + tab.shape[1:], lambda h, b: (0, 0, 0))
    in_specs = [
        qk_spec(q_head0),
        qk_spec(k_head0),
        pl.BlockSpec((None, vrows, SEQ), lambda h, b: (b, h, 0)),
        tab_spec,
    ] + extra_specs
    return pl.pallas_call(
        kernel,
        grid=(steps, batch),
        in_specs=in_specs,
        out_specs=pl.BlockSpec((outs_per_step, SEQ, LANES), lambda h, b: (h, b, 0)),
        out_shape=jax.ShapeDtypeStruct((steps * outs_per_step, m, LANES), BF16),
        compiler_params=pltpu.CompilerParams(
            dimension_semantics=("arbitrary", "arbitrary"), vmem_limit_bytes=VMEM_LIMIT),
        name=name,
    )(q_arr, k_arr, vt_arr, tab, *extra)


def _mix_kernel(x_ref, g_ref, oa_ref, ob_ref, oc_ref, wg_ref, wa_ref, wb_ref, wc_ref, wo_ref, o_ref):
    x = x_ref[...]
    h = _rms(x, g_ref[...]).astype(BF16)
    mix = None
    for n, (o_br, w_br) in enumerate(((oa_ref, wa_ref), (ob_ref, wb_ref), (oc_ref, wc_ref))):
        gate = jax.nn.sigmoid(jnp.dot(h, wg_ref[:, n * D_MODEL:(n + 1) * D_MODEL], preferred_element_type=F32))
        o = jnp.concatenate([o_br[hd] for hd in range(o_br.shape[0])], axis=1)
        term = gate * jnp.dot(o, w_br[...], preferred_element_type=F32)
        mix = term if mix is None else mix + term
    o_ref[...] = x + jnp.dot(mix.astype(BF16), wo_ref[...], preferred_element_type=F32)


def _mix(x2d, g, oa, ob, oc, wg, wa, wb, wc, wo, layer, tm):
    m = x2d.shape[0]
    row = lambda w: pl.BlockSpec((tm, w), lambda i: (i, 0))
    heads = lambda w: pl.BlockSpec((w // LANES, tm, LANES), lambda i: (0, i, 0))
    return pl.pallas_call(
        _mix_kernel,
        grid=(m // tm,),
        in_specs=[row(D_MODEL), _resident((1, D_MODEL)), heads(A_W), heads(B_W), heads(C_OUT_W),
                  _resident((D_MODEL, 3 * D_MODEL), layer), _resident((A_W, D_MODEL), layer),
                  _resident((B_W, D_MODEL), layer), _resident((C_OUT_W, D_MODEL), layer),
                  _resident((D_MODEL, D_MODEL), layer)],
        out_specs=row(D_MODEL),
        out_shape=jax.ShapeDtypeStruct((m, D_MODEL), F32),
        compiler_params=pltpu.CompilerParams(
            dimension_semantics=("arbitrary",), vmem_limit_bytes=VMEM_LIMIT),
        name="mix",
    )(x2d, g, oa, ob, oc, wg, wa, wb, wc, wo)


def _ffn_kernel(final, x_ref, g_ref, wg_ref, wu_ref, wd_ref, fg_ref, o_ref):
    half = x_ref.shape[0] // 2
    for part in range(2):
        rows = slice(part * half, (part + 1) * half)
        x = x_ref[rows, :]
        h = _rms(x, g_ref[...]).astype(BF16)
        gate = jnp.dot(h, wg_ref[...], preferred_element_type=F32)
        up = jnp.dot(h, wu_ref[...], preferred_element_type=F32)
        act = (gate * jax.nn.sigmoid(gate) * up).astype(BF16)
        y = x + jnp.dot(act, wd_ref[...], preferred_element_type=F32)
        o_ref[rows, :] = _rms(y, fg_ref[...]) if final else y


def _ffn(x2d, g, wg, wu, wd, fg, layer, final, tm):
    m = x2d.shape[0]
    return pl.pallas_call(
        functools.partial(_ffn_kernel, final),
        grid=(m // tm,),
        in_specs=[
            pl.BlockSpec((tm, D_MODEL), lambda i: (i, 0)),
            _resident((1, D_MODEL)),
            _resident((D_MODEL, FFN_HIDDEN), layer),
            _resident((D_MODEL, FFN_HIDDEN), layer),
            _resident((FFN_HIDDEN, D_MODEL), layer),
            _resident((1, D_MODEL)),
        ],
        out_specs=pl.BlockSpec((tm, D_MODEL), lambda i: (i, 0)),
        out_shape=jax.ShapeDtypeStruct((m, D_MODEL), F32),
        compiler_params=pltpu.CompilerParams(
            dimension_semantics=("arbitrary",), vmem_limit_bytes=VMEM_LIMIT),
        name="ffn",
    )(x2d, g, wg, wu, wd, fg)


def _alibi_slopes():
    n = A_HEADS + B_HEADS
    s = 2.0 ** (-8.0 * np.arange(1, n + 1, dtype=np.float64) / n)
    return s[0::2], s[1::2]


def _bias_tables(slopes, multiplicity, rows):
    r = np.arange(rows, dtype=np.int64)[:, None]
    i = np.arange(T_ATT, dtype=np.int64)[None, :]
    d = (rows - T_ATT) - r + i
    mult = multiplicity(d)
    valid = (d >= 0) & (mult > 0)
    tab = np.log2(np.maximum(mult, 1))[None] - (np.asarray(slopes) * LOG2E)[:, None, None] * d[None]
    return jnp.asarray(np.where(valid[None], tab, NEG_INF).astype(np.float32))


def _kpos_tables(slopes):
    v = (np.asarray(slopes) * LOG2E).astype(np.float32)[:, None] * np.arange(SEQ, dtype=np.float32)[None, :]
    tab = np.zeros(v.shape + (LANES,), BF16)
    for term in range(KPOS_TERMS):
        piece = v.astype(BF16)
        tab[..., term] = piece
        v = v - piece.astype(np.float32)
    return jnp.asarray(tab)


def _dilated_multiplicity(d):
    mult = np.zeros_like(d)
    for window, dil in B_PATTERNS:
        mult = mult + ((d % dil == 0) & (d // dil <= window // dil))
    return mult


def _rope_tables():
    half = C_ROPE_DIM // 2
    inv_freq = ROPE_THETA ** (-np.arange(half, dtype=np.float64) / half)
    ang = np.arange(SEQ, dtype=np.float64)[:, None] * inv_freq[None, :]
    cos = np.concatenate([np.cos(ang)] * 2, axis=-1)
    sin = np.concatenate([np.sin(ang)] * 2, axis=-1)
    scale = (C_NOPE_DIM + C_ROPE_DIM) ** -0.5 * LOG2E
    qtab = np.concatenate([np.full((SEQ, C_NOPE_DIM), scale), cos * scale, sin * scale], axis=-1)
    ktab = np.concatenate([np.zeros((SEQ, C_NOPE_DIM)), cos, sin], axis=-1)
    return jnp.asarray(qtab.astype(np.float32)), jnp.asarray(ktab.astype(np.float32))


def _rotate_half_cols(w):
    half = C_ROPE_DIM // 2
    return jnp.concatenate([-w[..., half:], w[..., :half]], axis=-1)


def _stacked_weights(w_in, w_uq, w_ukv):
    depth = w_in.shape[0]
    sa, sb = A_HEAD_DIM ** -0.5 * LOG2E, B_HEAD_DIM ** -0.5 * LOG2E
    b0 = 3 * A_W
    w_qk = jnp.concatenate([w_in[..., :2 * A_W], w_in[..., b0:b0 + 2 * B_W]], axis=-1).astype(BF16)
    w_gate = w_in[..., GATE_OFF:].astype(BF16)
    colscale = jnp.concatenate(
        [jnp.full((A_W,), sa, F32), jnp.ones((A_W,), F32), jnp.full((B_W,), sb, F32),
         jnp.ones((B_W,), F32)])[None, :]
    w_v = jnp.concatenate([w_in[..., 2 * A_W:3 * A_W], w_in[..., b0 + 2 * B_W:b0 + 3 * B_W]], axis=-1).astype(BF16)
    w_pe = w_in[..., C_PE_OFF:GATE_OFF]
    w_c = jnp.concatenate(
        [w_in[..., C_Q_OFF:C_PE_OFF], jnp.zeros((depth, D_MODEL, C_NOPE_DIM), F32), w_pe, _rotate_half_cols(w_pe)],
        axis=-1).astype(BF16)
    uq = w_uq.reshape(depth, C_Q_LORA, C_HEADS, C_NOPE_DIM + C_ROPE_DIM)
    uq_pe = uq[..., C_NOPE_DIM:]
    w_uq_ext = jnp.concatenate([uq, _rotate_half_cols(uq_pe)], axis=-1).reshape(
        depth, C_Q_LORA, C_HEADS * LANES).astype(BF16)
    ukv = w_ukv.reshape(depth, C_KV_LORA, C_HEADS, C_NOPE_DIM + C_V_DIM)
    w_uk = jnp.concatenate(
        [ukv[..., :C_NOPE_DIM], jnp.zeros((depth, C_KV_LORA, C_HEADS, LANES - C_NOPE_DIM), F32)],
        axis=-1).reshape(depth, C_KV_LORA, C_HEADS * LANES).astype(BF16)
    w_uv = ukv[..., C_NOPE_DIM:].reshape(depth, C_KV_LORA, C_OUT_W).astype(BF16)
    return w_qk, colscale, w_gate, w_v, w_c, w_uq_ext, w_uk, w_uv


def kernel(x, attn_norm, w_in, diff_lambda, diff_norm, mla_q_norm, mla_w_uq, mla_kv_norm, mla_w_ukv,
           w_branch_a, w_branch_b, w_branch_c, w_out, ffn_norm, w_ffn_gate, w_ffn_up, w_ffn_down,
           final_norm):
    batch, seq, d_model = x.shape
    assert (seq, d_model) == (SEQ, D_MODEL)
    m = batch * seq
    tm_proj, tm_ffn = 1024, 512

    slopes_a, slopes_b = _alibi_slopes()
    kpos_a = _kpos_tables(slopes_a)
    tab_b = _bias_tables(slopes_b, _dilated_multiplicity, SEQ)
    tab_c = _bias_tables(np.zeros((1,)), np.ones_like, T_ATT)
    qtab, ktab = _rope_tables()

    wa, wb, wc, wo = (w.astype(BF16) for w in (w_branch_a, w_branch_b, w_branch_c, w_out))
    wfg, wfu, wfd = (w.astype(BF16) for w in (w_ffn_gate, w_ffn_up, w_ffn_down))

    w_qk, colscale, w_gate, w_v, w_c, w_uq_ext, w_uk, w_uv = _stacked_weights(w_in, mla_w_uq, mla_w_ukv)

    x2d = x.reshape(m, D_MODEL)
    for l in range(DEPTH):
        g_attn = attn_norm[l][None, :]
        qk, vta, vtb, qc, kc, vtc = _proj(
            x2d, g_attn, w_qk, colscale, w_v, w_c, mla_q_norm[l][None, :], mla_kv_norm[l][None, :],
            w_uq_ext, w_uk, w_uv, qtab, ktab, l, batch, tm_proj)

        lam_init = 0.8 - 0.6 * math.exp(-0.3 * l)
        oa = _flash_call(
            functools.partial(_diff_attn_kernel, lam_init), "diff_attn", qk, 0, qk, A_HEADS, vta, tab_c,
            [kpos_a, diff_lambda[l], diff_norm[l][:, None]],
            [pl.BlockSpec((HEADS_PER_STEP, SEQ, LANES), lambda h, b: (h, 0, 0)),
             pl.BlockSpec((4, A_HEAD_DIM), lambda h, b: (0, 0)),
             pl.BlockSpec((2 * A_HEAD_DIM, 1), lambda h, b: (0, 0))],
            batch, A_HEADS // HEADS_PER_STEP, HEADS_PER_STEP, HEADS_PER_STEP, False)
        ob = _flash_call(_single_attn_kernel, "dilated_attn", qk, 2 * A_HEADS, qk, 2 * A_HEADS + B_HEADS, vtb, tab_b,
                         [], [], batch, B_HEADS // HEADS_PER_STEP, HEADS_PER_STEP, HEADS_PER_STEP, True)
        oc = _flash_call(_pair_attn_kernel, "mla_attn", qc, 0, kc, 0, vtc, tab_c,
                         [], [], batch, C_HEADS // (2 * HEADS_PER_STEP), 2 * HEADS_PER_STEP, HEADS_PER_STEP, False)

        x2d = _mix(x2d, g_attn, oa, ob, oc, w_gate, wa, wb, wc, wo, l, tm_proj)
        x2d = _ffn(x2d, ffn_norm[l][None, :], wfg, wfu, wfd, final_norm[None, :], l, l == DEPTH - 1, tm_ffn)
    return x2d.reshape(batch, seq, d_model)
```

```python
import functools
import math

import jax
import jax.numpy as jnp
import numpy as np
from jax import lax
from jax.experimental import pallas as pl
from jax.experimental.pallas import tpu as pltpu

D_MODEL = 1024
SEQ = 2048
DEPTH = 2
A_HEADS = 4
A_HEAD_DIM = 64
B_HEADS = 4
B_HEAD_DIM = 128
B_PATTERNS = ((128, 1), (512, 4), (2048, 16))
C_HEADS = 8
C_NOPE_DIM = 64
C_ROPE_DIM = 32
C_V_DIM = 64
C_Q_LORA = 384
C_KV_LORA = 256
ROPE_THETA = 10000.0
FFN_HIDDEN = 2816
A_W = A_HEADS * 2 * A_HEAD_DIM
B_W = B_HEADS * B_HEAD_DIM
C_OUT_W = C_HEADS * C_V_DIM
C_Q_OFF = 2 * A_W + A_W + 3 * B_W
C_KV_OFF = C_Q_OFF + C_Q_LORA
C_PE_OFF = C_KV_OFF + C_KV_LORA
GATE_OFF = C_PE_OFF + C_ROPE_DIM
NORM_EPS = 1e-6
NEG_INF = -1e30
LOG2E = 1.4426950408889634

LANES = 128
T_ATT = 256
VMEM_LIMIT = 48 * 1024 * 1024

BF16 = jnp.bfloat16
F32 = jnp.float32


def _rms(xf, g):
    return xf * lax.rsqrt(jnp.mean(xf * xf, axis=-1, keepdims=True) + NORM_EPS) * g


def _nt_dot(a, b):
    return lax.dot_general(a, b, (((1,), (1,)), ((), ())), preferred_element_type=F32)


def _tn_dot(a, b):
    return lax.dot_general(a, b, (((0,), (1,)), ((), ())), preferred_element_type=F32)


QK_W = 2 * A_W + 2 * B_W
C_PE_GROUP = C_Q_LORA + C_KV_LORA
C_CW = C_PE_GROUP + LANES


def _resident(shape, layer=None):
    if layer is None:
        return pl.BlockSpec(shape, lambda *_: (0,) * len(shape), pipeline_mode=pl.Buffered(1))
    return pl.BlockSpec((None,) + shape, lambda *_: (layer,) + (0,) * len(shape),
                        pipeline_mode=pl.Buffered(1))


def _proj_kernel(x_ref, g_ref, wqk_ref, cs_ref, wv_ref, wc_ref, qn_ref, kvn_ref, wuq_ref, wuk_ref,
                 wuv_ref, qtab_ref, ktab_ref, qk_ref, vta_ref, vtb_ref, qc_ref, kc_ref, vtc_ref):
    h = _rms(x_ref[...], g_ref[...]).astype(BF16)

    qk = (_nt_dot(h, wqk_ref[...]) * cs_ref[...]).astype(BF16)
    for hd in range(QK_W // LANES):
        qk_ref[hd] = qk[:, hd * LANES:(hd + 1) * LANES]

    vt = _nt_dot(wv_ref[...], h).astype(BF16)
    vta_ref[...] = vt[:A_W]
    vtb_ref[...] = vt[A_W:]

    c = _nt_dot(h, wc_ref[...])
    cqn = _rms(c[:, :C_Q_LORA], qn_ref[...]).astype(BF16)
    ckvn = _rms(c[:, C_Q_LORA:C_PE_GROUP], kvn_ref[...]).astype(BF16)

    q = jnp.dot(cqn, wuq_ref[...], preferred_element_type=F32)
    qtab = qtab_ref[...]
    for hd in range(C_HEADS):
        sl = slice(hd * LANES, (hd + 1) * LANES)
        qc_ref[hd] = (q[:, sl] * qtab).astype(BF16)

    kt = c[:, C_PE_GROUP:] * ktab_ref[...]
    lane = lax.broadcasted_iota(jnp.int32, kt.shape, 1)
    swapped = jnp.where(lane < C_NOPE_DIM + C_ROPE_DIM, pltpu.roll(kt, LANES - C_ROPE_DIM, 1),
                        pltpu.roll(kt, C_ROPE_DIM, 1))
    kp2 = jnp.where(lane >= C_NOPE_DIM, kt + swapped, 0.0)
    kk = jnp.dot(ckvn, wuk_ref[...], preferred_element_type=F32)
    for hd in range(C_HEADS):
        sl = slice(hd * LANES, (hd + 1) * LANES)
        kc_ref[hd] = (kk[:, sl] + kp2).astype(BF16)

    vtc_ref[...] = _tn_dot(wuv_ref[...], ckvn).astype(BF16)


def _proj(x2d, g, wqk, colscale, wv, wc, qn, kvn, wuq, wuk, wuv, qtab, ktab, layer, batch, tm):
    m = x2d.shape[0]
    per_seq = SEQ // tm
    row = lambda w: pl.BlockSpec((tm, w), lambda i: (i, 0))
    pos = pl.BlockSpec((tm, LANES), lambda i: (i % per_seq, 0))
    vt_spec = lambda w: pl.BlockSpec((None, w, tm), lambda i: (i // per_seq, 0, i % per_seq))
    heads_spec = lambda nh: pl.BlockSpec((nh, tm, LANES), lambda i: (0, i, 0))
    return pl.pallas_call(
        _proj_kernel,
        grid=(m // tm,),
        in_specs=[
            row(D_MODEL),
            _resident((1, D_MODEL)),
            _resident((QK_W, D_MODEL), layer),
            _resident((1, QK_W)),
            _resident((A_W + B_W, D_MODEL), layer),
            _resident((C_CW, D_MODEL), layer),
            _resident((1, C_Q_LORA)),
            _resident((1, C_KV_LORA)),
            _resident((C_Q_LORA, C_HEADS * LANES), layer),
            _resident((C_KV_LORA, C_HEADS * LANES), layer),
            _resident((C_KV_LORA, C_OUT_W), layer),
            pos, pos,
        ],
        out_specs=[heads_spec(QK_W // LANES), vt_spec(A_W), vt_spec(B_W), heads_spec(C_HEADS),
                   heads_spec(C_HEADS), vt_spec(C_OUT_W)],
        out_shape=[
            jax.ShapeDtypeStruct((QK_W // LANES, m, LANES), BF16),
            jax.ShapeDtypeStruct((batch, A_W, SEQ), BF16),
            jax.ShapeDtypeStruct((batch, B_W, SEQ), BF16),
            jax.ShapeDtypeStruct((C_HEADS, m, LANES), BF16),
            jax.ShapeDtypeStruct((C_HEADS, m, LANES), BF16),
            jax.ShapeDtypeStruct((batch, C_OUT_W, SEQ), BF16),
        ],
        compiler_params=pltpu.CompilerParams(
            dimension_semantics=("arbitrary",), vmem_limit_bytes=VMEM_LIMIT),
        name="proj",
    )(x2d, g, wqk, colscale, wv, wc, qn, kvn, wuq, wuk, wuv, qtab, ktab)


NQ_ATT = SEQ // T_ATT
KPOS_TERMS = 3
HEADS_PER_STEP = 4
SUM_ROWS = 16


def _key_tile(jt):
    return slice(jt * T_ATT, (jt + 1) * T_ATT)


def _causal_exp2(st, m):
    n, half = st.shape[0], T_ATT // 2
    live = jnp.exp2(st[:n - half] - m).astype(BF16)
    corner = jnp.exp2(st[n - half:, half:] - m[:, half:]).astype(BF16)
    dead = jnp.zeros((half, half), BF16)
    return jnp.concatenate([live, jnp.concatenate([dead, corner], axis=1)], axis=0)


def _attend(jobs):
    tiles = [(job, t) for job in jobs for t in range(NQ_ATT)]
    scored = None
    probs = None
    for step in range(len(tiles) + 2):
        new_scored = None
        if step < len(tiles):
            job, t = tiles[step]
            qs_of, k_rows, _, tab_ref, bias_off_diag, _ = job
            n = (t + 1) * T_ATT
            outs = []
            for s, q in enumerate(qs_of(t)):
                st = _nt_dot(k_rows(s, n), q)
                if bias_off_diag:
                    st = st + tab_ref[tab_ref.shape[0] - n:, :]
                elif t == 0:
                    st = st + tab_ref[...]
                else:
                    st = jnp.concatenate([st[:n - T_ATT], st[n - T_ATT:] + tab_ref[...]], axis=0)
                outs.append((st, jnp.max(st, axis=0, keepdims=True)))
            new_scored = (job, t, outs)
        new_probs = None
        if scored is not None:
            new_probs = scored[:2] + ([_causal_exp2(st, m) for st, m in scored[2]],)
        if probs is not None:
            job, t, ps = probs
            _, _, vt_cols, _, _, finish = job
            n = (t + 1) * T_ATT
            ones_rows = (lax.broadcasted_iota(jnp.int32, (SUM_ROWS, n), 0) == 0).astype(BF16)
            outs = []
            for s, p in enumerate(ps):
                vt = jnp.concatenate([vt_cols(s, n), ones_rows], axis=0)
                a = jnp.dot(vt, p, preferred_element_type=F32)
                outs.append((a[:-SUM_ROWS], a[-SUM_ROWS:-SUM_ROWS + 1]))
            finish(t, outs)
        scored, probs = new_scored, new_probs


def _diff_attn_kernel(lam_init, q_ref, k_ref, vt_ref, tab_ref, kpos_ref, dl_ref, gain_ref, o_ref):
    dl = dl_ref[...]
    lam = (jnp.exp(jnp.sum(dl[0:1] * dl[1:2], axis=-1, keepdims=True))
           - jnp.exp(jnp.sum(dl[2:3] * dl[3:4], axis=-1, keepdims=True)) + lam_init)
    gain = gain_ref[...] * (1.0 - lam_init)
    lane = lax.broadcasted_iota(jnp.int32, (T_ATT, LANES), 1)
    q_ones = (lane < KPOS_TERMS).astype(BF16)
    dv = 2 * A_HEAD_DIM

    def job(g):
        def qs_of(qi):
            q = q_ref[g, _key_tile(qi), :]
            zero = jnp.zeros_like(q)
            return [jnp.concatenate([jnp.where(lane < A_HEAD_DIM, q, zero), q_ones], axis=1),
                    jnp.concatenate([jnp.where(lane >= A_HEAD_DIM, q, zero), q_ones], axis=1)]

        def finish(qi, outs):
            (a0, l0), (a1, l1) = outs
            o = a0 / l0 - lam * (a1 / l1)
            y = o * lax.rsqrt(jnp.mean(o * o, axis=0, keepdims=True) + NORM_EPS) * gain
            o_ref[g, _key_tile(qi), :] = y.T.astype(BF16)

        return (qs_of, lambda s, n: jnp.concatenate([k_ref[g, :n, :], kpos_ref[g, :n, :]], axis=1),
                lambda s, n: vt_ref[g * dv:(g + 1) * dv, :n], tab_ref, False, finish)

    _attend([job(g) for g in range(q_ref.shape[0])])


def _single_attn_kernel(q_ref, k_ref, vt_ref, tab_ref, o_ref):
    def job(g):
        def finish(qi, outs):
            (a, l), = outs
            o_ref[g, _key_tile(qi), :] = (a / l).T.astype(BF16)

        return (lambda qi: [q_ref[g, _key_tile(qi), :]], lambda s, n: k_ref[g, :n, :],
                lambda s, n: vt_ref[g * B_HEAD_DIM:(g + 1) * B_HEAD_DIM, :n], tab_ref.at[g], True, finish)

    _attend([job(g) for g in range(q_ref.shape[0])])


def _pair_attn_kernel(q_ref, k_ref, vt_ref, tab_ref, o_ref):
    def job(g):
        def finish(qi, outs):
            (a0, l0), (a1, l1) = outs
            o_ref[g, _key_tile(qi), :] = jnp.concatenate([a0 / l0, a1 / l1], axis=0).T.astype(BF16)

        return (lambda qi: [q_ref[2 * g, _key_tile(qi), :], q_ref[2 * g + 1, _key_tile(qi), :]],
                lambda s, n: k_ref[2 * g + s, :n, :],
                lambda s, n: vt_ref[(2 * g + s) * C_V_DIM:(2 * g + s + 1) * C_V_DIM, :n],
                tab_ref, False, finish)

    _attend([job(g) for g in range(o_ref.shape[0])])


def _flash_call(kernel, name, q_arr, q_head0, k_arr, k_head0, vt_arr, tab, extra, extra_specs,
                batch, steps, heads_per_step, outs_per_step, tab_per_head):
    m = q_arr.shape[1]
    vrows = vt_arr.shape[1] // steps
    qk_spec = lambda h0: pl.BlockSpec((heads_per_step, SEQ, LANES),
                                      lambda h, b: (h0 // heads_per_step + h, b, 0))
    if tab_per_head:
        tab_spec = pl.BlockSpec((heads_per_step,) + tab.shape[1:], lambda h, b: (h, 0, 0))
    else:
        tab_spec = pl.BlockSpec((None,) + tab.shape[1:], lambda h, b: (0, 0, 0))
    in_specs = [
        qk_spec(q_head0),
        qk_spec(k_head0),
        pl.BlockSpec((None, vrows, SEQ), lambda h, b: (b, h, 0)),
        tab_spec,
    ] + extra_specs
    return pl.pallas_call(
        kernel,
        grid=(steps, batch),
        in_specs=in_specs,
        out_specs=pl.BlockSpec((outs_per_step, SEQ, LANES), lambda h, b: (h, b, 0)),
        out_shape=jax.ShapeDtypeStruct((steps * outs_per_step, m, LANES), BF16),
        compiler_params=pltpu.CompilerParams(
            dimension_semantics=("arbitrary", "arbitrary"), vmem_limit_bytes=VMEM_LIMIT),
        name=name,
    )(q_arr, k_arr, vt_arr, tab, *extra)


def _mix_kernel(x_ref, g_ref, oa_ref, ob_ref, oc_ref, wg_ref, wa_ref, wb_ref, wc_ref, wo_ref, o_ref):
    x = x_ref[...]
    h = _rms(x, g_ref[...]).astype(BF16)
    mix = None
    for n, (o_br, w_br) in enumerate(((oa_ref, wa_ref), (ob_ref, wb_ref), (oc_ref, wc_ref))):
        gate = jax.nn.sigmoid(_nt_dot(h, wg_ref[n * D_MODEL:(n + 1) * D_MODEL, :]))
        o = jnp.concatenate([o_br[hd] for hd in range(o_br.shape[0])], axis=1)
        term = gate * jnp.dot(o, w_br[...], preferred_element_type=F32)
        mix = term if mix is None else mix + term
    o_ref[...] = x + jnp.dot(mix.astype(BF16), wo_ref[...], preferred_element_type=F32)


def _mix(x2d, g, oa, ob, oc, wg, wa, wb, wc, wo, layer, tm):
    m = x2d.shape[0]
    row = lambda w: pl.BlockSpec((tm, w), lambda i: (i, 0))
    heads = lambda w: pl.BlockSpec((w // LANES, tm, LANES), lambda i: (0, i, 0))
    return pl.pallas_call(
        _mix_kernel,
        grid=(m // tm,),
        in_specs=[row(D_MODEL), _resident((1, D_MODEL)), heads(A_W), heads(B_W), heads(C_OUT_W),
                  _resident((3 * D_MODEL, D_MODEL), layer), _resident((A_W, D_MODEL), layer),
                  _resident((B_W, D_MODEL), layer), _resident((C_OUT_W, D_MODEL), layer),
                  _resident((D_MODEL, D_MODEL), layer)],
        out_specs=row(D_MODEL),
        out_shape=jax.ShapeDtypeStruct((m, D_MODEL), F32),
        compiler_params=pltpu.CompilerParams(
            dimension_semantics=("arbitrary",), vmem_limit_bytes=VMEM_LIMIT),
        name="mix",
    )(x2d, g, oa, ob, oc, wg, wa, wb, wc, wo)


def _ffn_kernel(final, x_ref, g_ref, wg_ref, wu_ref, wd_ref, fg_ref, o_ref):
    half = x_ref.shape[0] // 2
    for part in range(2):
        rows = slice(part * half, (part + 1) * half)
        x = x_ref[rows, :]
        h = _rms(x, g_ref[...]).astype(BF16)
        gate = jnp.dot(h, wg_ref[...], preferred_element_type=F32)
        up = jnp.dot(h, wu_ref[...], preferred_element_type=F32)
        act = (gate * jax.nn.sigmoid(gate) * up).astype(BF16)
        y = x + jnp.dot(act, wd_ref[...], preferred_element_type=F32)
        o_ref[rows, :] = _rms(y, fg_ref[...]) if final else y


def _ffn(x2d, g, wg, wu, wd, fg, layer, final, tm):
    m = x2d.shape[0]
    return pl.pallas_call(
        functools.partial(_ffn_kernel, final),
        grid=(m // tm,),
        in_specs=[
            pl.BlockSpec((tm, D_MODEL), lambda i: (i, 0)),
            _resident((1, D_MODEL)),
            _resident((D_MODEL, FFN_HIDDEN), layer),
            _resident((D_MODEL, FFN_HIDDEN), layer),
            _resident((FFN_HIDDEN, D_MODEL), layer),
            _resident((1, D_MODEL)),
        ],
        out_specs=pl.BlockSpec((tm, D_MODEL), lambda i: (i, 0)),
        out_shape=jax.ShapeDtypeStruct((m, D_MODEL), F32),
        compiler_params=pltpu.CompilerParams(
            dimension_semantics=("arbitrary",), vmem_limit_bytes=VMEM_LIMIT),
        name="ffn",
    )(x2d, g, wg, wu, wd, fg)


def _alibi_slopes():
    n = A_HEADS + B_HEADS
    s = 2.0 ** (-8.0 * np.arange(1, n + 1, dtype=np.float64) / n)
    return s[0::2], s[1::2]


def _bias_tables(slopes, multiplicity, rows):
    r = np.arange(rows, dtype=np.int64)[:, None]
    i = np.arange(T_ATT, dtype=np.int64)[None, :]
    d = (rows - T_ATT) - r + i
    mult = multiplicity(d)
    valid = (d >= 0) & (mult > 0)
    tab = np.log2(np.maximum(mult, 1))[None] - (np.asarray(slopes) * LOG2E)[:, None, None] * d[None]
    return jnp.asarray(np.where(valid[None], tab, NEG_INF).astype(np.float32))


def _kpos_tables(slopes):
    v = (np.asarray(slopes) * LOG2E).astype(np.float32)[:, None] * np.arange(SEQ, dtype=np.float32)[None, :]
    tab = np.zeros(v.shape + (LANES,), BF16)
    for term in range(KPOS_TERMS):
        piece = v.astype(BF16)
        tab[..., term] = piece
        v = v - piece.astype(np.float32)
    return jnp.asarray(tab)


def _dilated_multiplicity(d):
    mult = np.zeros_like(d)
    for window, dil in B_PATTERNS:
        mult = mult + ((d % dil == 0) & (d // dil <= window // dil))
    return mult


def _rope_tables():
    half = C_ROPE_DIM // 2
    inv_freq = ROPE_THETA ** (-np.arange(half, dtype=np.float64) / half)
    ang = np.arange(SEQ, dtype=np.float64)[:, None] * inv_freq[None, :]
    cos = np.concatenate([np.cos(ang)] * 2, axis=-1)
    sin = np.concatenate([np.sin(ang)] * 2, axis=-1)
    scale = (C_NOPE_DIM + C_ROPE_DIM) ** -0.5 * LOG2E
    qtab = np.concatenate([np.full((SEQ, C_NOPE_DIM), scale), cos * scale, sin * scale], axis=-1)
    ktab = np.concatenate([np.zeros((SEQ, C_NOPE_DIM)), cos, sin], axis=-1)
    return jnp.asarray(qtab.astype(np.float32)), jnp.asarray(ktab.astype(np.float32))


def _rotate_half(w, axis):
    first, second = jnp.split(w, 2, axis=axis)
    return jnp.concatenate([-second, first], axis=axis)


def _stacked_weights(w_in, w_uq, w_ukv):
    depth = w_in.shape[0]
    sa, sb = A_HEAD_DIM ** -0.5 * LOG2E, B_HEAD_DIM ** -0.5 * LOG2E
    b0 = 3 * A_W
    w_t = jnp.swapaxes(w_in, -1, -2)
    w_qk = jnp.concatenate([w_t[:, :2 * A_W], w_t[:, b0:b0 + 2 * B_W]], axis=1).astype(BF16)
    w_gate = w_t[:, GATE_OFF:].astype(BF16)
    colscale = jnp.concatenate(
        [jnp.full((A_W,), sa, F32), jnp.ones((A_W,), F32), jnp.full((B_W,), sb, F32),
         jnp.ones((B_W,), F32)])[None, :]
    w_v = jnp.concatenate([w_t[:, 2 * A_W:3 * A_W], w_t[:, b0 + 2 * B_W:b0 + 3 * B_W]], axis=1).astype(BF16)
    w_pe = w_t[:, C_PE_OFF:GATE_OFF]
    w_c = jnp.concatenate(
        [w_t[:, C_Q_OFF:C_PE_OFF], jnp.zeros((depth, C_NOPE_DIM, D_MODEL), F32), w_pe, _rotate_half(w_pe, 1)],
        axis=1).astype(BF16)
    uq = w_uq.reshape(depth, C_Q_LORA, C_HEADS, C_NOPE_DIM + C_ROPE_DIM)
    uq_pe = uq[..., C_NOPE_DIM:]
    w_uq_ext = jnp.concatenate([uq, _rotate_half(uq_pe, -1)], axis=-1).reshape(
        depth, C_Q_LORA, C_HEADS * LANES).astype(BF16)
    ukv = w_ukv.reshape(depth, C_KV_LORA, C_HEADS, C_NOPE_DIM + C_V_DIM)
    w_uk = jnp.concatenate(
        [ukv[..., :C_NOPE_DIM], jnp.zeros((depth, C_KV_LORA, C_HEADS, LANES - C_NOPE_DIM), F32)],
        axis=-1).reshape(depth, C_KV_LORA, C_HEADS * LANES).astype(BF16)
    w_uv = ukv[..., C_NOPE_DIM:].reshape(depth, C_KV_LORA, C_OUT_W).astype(BF16)
    return w_qk, colscale, w_gate, w_v, w_c, w_uq_ext, w_uk, w_uv


def kernel(x, attn_norm, w_in, diff_lambda, diff_norm, mla_q_norm, mla_w_uq, mla_kv_norm, mla_w_ukv,
           w_branch_a, w_branch_b, w_branch_c, w_out, ffn_norm, w_ffn_gate, w_ffn_up, w_ffn_down,
           final_norm):
    batch, seq, d_model = x.shape
    assert (seq, d_model) == (SEQ, D_MODEL)
    m = batch * seq
    tm_proj, tm_ffn = 1024, 512

    slopes_a, slopes_b = _alibi_slopes()
    kpos_a = _kpos_tables(slopes_a)
    tab_b = _bias_tables(slopes_b, _dilated_multiplicity, SEQ)
    tab_c = _bias_tables(np.zeros((1,)), np.ones_like, T_ATT)
    qtab, ktab = _rope_tables()

    wa, wb, wc, wo = (w.astype(BF16) for w in (w_branch_a, w_branch_b, w_branch_c, w_out))
    wfg, wfu, wfd = (w.astype(BF16) for w in (w_ffn_gate, w_ffn_up, w_ffn_down))

    w_qk, colscale, w_gate, w_v, w_c, w_uq_ext, w_uk, w_uv = _stacked_weights(w_in, mla_w_uq, mla_w_ukv)

    x2d = x.reshape(m, D_MODEL)
    for l in range(DEPTH):
        g_attn = attn_norm[l][None, :]
        qk, vta, vtb, qc, kc, vtc = _proj(
            x2d, g_attn, w_qk, colscale, w_v, w_c, mla_q_norm[l][None, :], mla_kv_norm[l][None, :],
            w_uq_ext, w_uk, w_uv, qtab, ktab, l, batch, tm_proj)

        lam_init = 0.8 - 0.6 * math.exp(-0.3 * l)
        oa = _flash_call(
            functools.partial(_diff_attn_kernel, lam_init), "diff_attn", qk, 0, qk, A_HEADS, vta, tab_c,
            [kpos_a, diff_lambda[l], diff_norm[l][:, None]],
            [pl.BlockSpec((HEADS_PER_STEP, SEQ, LANES), lambda h, b: (h, 0, 0)),
             pl.BlockSpec((4, A_HEAD_DIM), lambda h, b: (0, 0)),
             pl.BlockSpec((2 * A_HEAD_DIM, 1), lambda h, b: (0, 0))],
            batch, A_HEADS // HEADS_PER_STEP, HEADS_PER_STEP, HEADS_PER_STEP, False)
        ob = _flash_call(_single_attn_kernel, "dilated_attn", qk, 2 * A_HEADS, qk, 2 * A_HEADS + B_HEADS, vtb, tab_b,
                         [], [], batch, B_HEADS // HEADS_PER_STEP, HEADS_PER_STEP, HEADS_PER_STEP, True)
        oc = _flash_call(_pair_attn_kernel, "mla_attn", qc, 0, kc, 0, vtc, tab_c,
                         [], [], batch, C_HEADS // (2 * HEADS_PER_STEP), 2 * HEADS_PER_STEP, HEADS_PER_STEP, False)

        x2d = _mix(x2d, g_attn, oa, ob, oc, w_gate, wa, wb, wc, wo, l, tm_proj)
        x2d = _ffn(x2d, ffn_norm[l][None, :], wfg, wfu, wfd, final_norm[None, :], l, l == DEPTH - 1, tm_ffn)
    return x2d.reshape(batch, seq, d_model)
```

```python
import functools
import math

import jax
import jax.numpy as jnp
import numpy as np
from jax import lax
from jax.experimental import pallas as pl
from jax.experimental.pallas import tpu as pltpu

D_MODEL = 1024
SEQ = 2048
DEPTH = 2
A_HEADS = 4
A_HEAD_DIM = 64
B_HEADS = 4
B_HEAD_DIM = 128
B_PATTERNS = ((128, 1), (512, 4), (2048, 16))
C_HEADS = 8
C_NOPE_DIM = 64
C_ROPE_DIM = 32
C_V_DIM = 64
C_Q_LORA = 384
C_KV_LORA = 256
ROPE_THETA = 10000.0
FFN_HIDDEN = 2816
A_W = A_HEADS * 2 * A_HEAD_DIM
B_W = B_HEADS * B_HEAD_DIM
C_OUT_W = C_HEADS * C_V_DIM
C_Q_OFF = 2 * A_W + A_W + 3 * B_W
C_KV_OFF = C_Q_OFF + C_Q_LORA
C_PE_OFF = C_KV_OFF + C_KV_LORA
GATE_OFF = C_PE_OFF + C_ROPE_DIM
NORM_EPS = 1e-6
NEG_INF = -1e30
LOG2E = 1.4426950408889634

LANES = 128
T_ATT = 256
VMEM_LIMIT = 48 * 1024 * 1024

BF16 = jnp.bfloat16
F32 = jnp.float32


def _rms(xf, g):
    return xf * lax.rsqrt(jnp.mean(xf * xf, axis=-1, keepdims=True) + NORM_EPS) * g


def _nt_dot(a, b):
    return lax.dot_general(a, b, (((1,), (1,)), ((), ())), preferred_element_type=F32)


def _tn_dot(a, b):
    return lax.dot_general(a, b, (((0,), (1,)), ((), ())), preferred_element_type=F32)


QK_W = 2 * A_W + 2 * B_W
C_PE_GROUP = C_Q_LORA + C_KV_LORA
C_CW = C_PE_GROUP + LANES


def _resident(shape, layer=None):
    if layer is None:
        return pl.BlockSpec(shape, lambda *_: (0,) * len(shape), pipeline_mode=pl.Buffered(1))
    return pl.BlockSpec((None,) + shape, lambda *_: (layer,) + (0,) * len(shape),
                        pipeline_mode=pl.Buffered(1))


def _proj_kernel(x_ref, g_ref, wqk_ref, cs_ref, wv_ref, wc_ref, qn_ref, kvn_ref, wuq_ref, wuk_ref,
                 wuv_ref, qtab_ref, ktab_ref, qk_ref, vta_ref, vtb_ref, qc_ref, kc_ref, vtc_ref):
    h = _rms(x_ref[...], g_ref[...]).astype(BF16)

    qk = (_nt_dot(h, wqk_ref[...]) * cs_ref[...]).astype(BF16)
    for hd in range(QK_W // LANES):
        qk_ref[hd] = qk[:, hd * LANES:(hd + 1) * LANES]

    vt = _nt_dot(wv_ref[...], h).astype(BF16)
    vta_ref[...] = vt[:A_W]
    vtb_ref[...] = vt[A_W:]

    c = _nt_dot(h, wc_ref[...])
    cqn = _rms(c[:, :C_Q_LORA], qn_ref[...]).astype(BF16)
    ckvn = _rms(c[:, C_Q_LORA:C_PE_GROUP], kvn_ref[...]).astype(BF16)

    q = jnp.dot(cqn, wuq_ref[...], preferred_element_type=F32)
    qtab = qtab_ref[...]
    for hd in range(C_HEADS):
        sl = slice(hd * LANES, (hd + 1) * LANES)
        qc_ref[hd] = (q[:, sl] * qtab).astype(BF16)

    kt = c[:, C_PE_GROUP:] * ktab_ref[...]
    lane = lax.broadcasted_iota(jnp.int32, kt.shape, 1)
    swapped = jnp.where(lane < C_NOPE_DIM + C_ROPE_DIM, pltpu.roll(kt, LANES - C_ROPE_DIM, 1),
                        pltpu.roll(kt, C_ROPE_DIM, 1))
    kp2 = jnp.where(lane >= C_NOPE_DIM, kt + swapped, 0.0)
    kk = jnp.dot(ckvn, wuk_ref[...], preferred_element_type=F32)
    for hd in range(C_HEADS):
        sl = slice(hd * LANES, (hd + 1) * LANES)
        kc_ref[hd] = (kk[:, sl] + kp2).astype(BF16)

    vtc_ref[...] = _tn_dot(wuv_ref[...], ckvn).astype(BF16)


def _proj(x2d, g, wqk, colscale, wv, wc, qn, kvn, wuq, wuk, wuv, qtab, ktab, layer, batch, tm):
    m = x2d.shape[0]
    per_seq = SEQ // tm
    row = lambda w: pl.BlockSpec((tm, w), lambda i: (i, 0))
    pos = pl.BlockSpec((tm, LANES), lambda i: (i % per_seq, 0))
    vt_spec = lambda w: pl.BlockSpec((None, w, tm), lambda i: (i // per_seq, 0, i % per_seq))
    heads_spec = lambda nh: pl.BlockSpec((nh, tm, LANES), lambda i: (0, i, 0))
    return pl.pallas_call(
        _proj_kernel,
        grid=(m // tm,),
        in_specs=[
            row(D_MODEL),
            _resident((1, D_MODEL)),
            _resident((QK_W, D_MODEL), layer),
            _resident((1, QK_W)),
            _resident((A_W + B_W, D_MODEL), layer),
            _resident((C_CW, D_MODEL), layer),
            _resident((1, C_Q_LORA)),
            _resident((1, C_KV_LORA)),
            _resident((C_Q_LORA, C_HEADS * LANES), layer),
            _resident((C_KV_LORA, C_HEADS * LANES), layer),
            _resident((C_KV_LORA, C_OUT_W), layer),
            pos, pos,
        ],
        out_specs=[heads_spec(QK_W // LANES), vt_spec(A_W), vt_spec(B_W), heads_spec(C_HEADS),
                   heads_spec(C_HEADS), vt_spec(C_OUT_W)],
        out_shape=[
            jax.ShapeDtypeStruct((QK_W // LANES, m, LANES), BF16),
            jax.ShapeDtypeStruct((batch, A_W, SEQ), BF16),
            jax.ShapeDtypeStruct((batch, B_W, SEQ), BF16),
            jax.ShapeDtypeStruct((C_HEADS, m, LANES), BF16),
            jax.ShapeDtypeStruct((C_HEADS, m, LANES), BF16),
            jax.ShapeDtypeStruct((batch, C_OUT_W, SEQ), BF16),
        ],
        compiler_params=pltpu.CompilerParams(
            dimension_semantics=("arbitrary",), vmem_limit_bytes=VMEM_LIMIT),
        name="proj",
    )(x2d, g, wqk, colscale, wv, wc, qn, kvn, wuq, wuk, wuv, qtab, ktab)


NQ_ATT = SEQ // T_ATT
KPOS_TERMS = 3
HEADS_PER_STEP = 4
SUM_ROWS = 16


def _key_tile(jt):
    return slice(jt * T_ATT, (jt + 1) * T_ATT)


def _causal_exp2(st, m, exp_dtype):
    n, half = st.shape[0], T_ATT // 2
    live = jnp.exp2((st[:n - half] - m).astype(exp_dtype)).astype(BF16)
    corner = jnp.exp2((st[n - half:, half:] - m[:, half:]).astype(exp_dtype)).astype(BF16)
    dead = jnp.zeros((half, half), BF16)
    return jnp.concatenate([live, jnp.concatenate([dead, corner], axis=1)], axis=0)


def _attend(jobs):
    tiles = [(job, t) for job in jobs for t in range(NQ_ATT)]
    scored = None
    probs = None
    for step in range(len(tiles) + 2):
        new_scored = None
        if step < len(tiles):
            job, t = tiles[step]
            qs_of, k_rows, _, tab_ref, bias_off_diag, _, _ = job
            n = (t + 1) * T_ATT
            outs = []
            for s, q in enumerate(qs_of(t)):
                st = _nt_dot(k_rows(s, n), q)
                if bias_off_diag:
                    st = st + tab_ref[tab_ref.shape[0] - n:, :]
                elif t == 0:
                    st = st + tab_ref[...]
                else:
                    st = jnp.concatenate([st[:n - T_ATT], st[n - T_ATT:] + tab_ref[...]], axis=0)
                outs.append((st, jnp.max(st, axis=0, keepdims=True)))
            new_scored = (job, t, outs)
        new_probs = None
        if scored is not None:
            new_probs = scored[:2] + ([_causal_exp2(st, m, scored[0][6]) for st, m in scored[2]],)
        if probs is not None:
            job, t, ps = probs
            _, _, vt_cols, _, _, finish, _ = job
            n = (t + 1) * T_ATT
            ones_rows = (lax.broadcasted_iota(jnp.int32, (SUM_ROWS, n), 0) == 0).astype(BF16)
            outs = []
            for s, p in enumerate(ps):
                vt = jnp.concatenate([vt_cols(s, n), ones_rows], axis=0)
                a = jnp.dot(vt, p, preferred_element_type=F32)
                outs.append((a[:-SUM_ROWS], a[-SUM_ROWS:-SUM_ROWS + 1]))
            finish(t, outs)
        scored, probs = new_scored, new_probs


def _diff_attn_kernel(lam_init, q_ref, k_ref, vt_ref, tab_ref, kpos_ref, dl_ref, gain_ref, o_ref):
    dl = dl_ref[...]
    lam = (jnp.exp(jnp.sum(dl[0:1] * dl[1:2], axis=-1, keepdims=True))
           - jnp.exp(jnp.sum(dl[2:3] * dl[3:4], axis=-1, keepdims=True)) + lam_init)
    gain = gain_ref[...] * (1.0 - lam_init)
    lane = lax.broadcasted_iota(jnp.int32, (T_ATT, LANES), 1)
    q_ones = (lane < KPOS_TERMS).astype(BF16)
    dv = 2 * A_HEAD_DIM

    def job(g):
        def qs_of(qi):
            q = q_ref[g, _key_tile(qi), :]
            zero = jnp.zeros_like(q)
            return [jnp.concatenate([jnp.where(lane < A_HEAD_DIM, q, zero), q_ones], axis=1),
                    jnp.concatenate([jnp.where(lane >= A_HEAD_DIM, q, zero), q_ones], axis=1)]

        def finish(qi, outs):
            (a0, l0), (a1, l1) = outs
            o = a0 / l0 - lam * (a1 / l1)
            y = o * lax.rsqrt(jnp.mean(o * o, axis=0, keepdims=True) + NORM_EPS) * gain
            o_ref[g, _key_tile(qi), :] = y.T.astype(BF16)

        return (qs_of, lambda s, n: jnp.concatenate([k_ref[g, :n, :], kpos_ref[g, :n, :]], axis=1),
                lambda s, n: vt_ref[g * dv:(g + 1) * dv, :n], tab_ref, False, finish, F32)

    _attend([job(g) for g in range(q_ref.shape[0])])


def _single_attn_kernel(q_ref, k_ref, vt_ref, tab_ref, o_ref):
    def job(g):
        def finish(qi, outs):
            (a, l), = outs
            o_ref[g, _key_tile(qi), :] = (a / l).T.astype(BF16)

        return (lambda qi: [q_ref[g, _key_tile(qi), :]], lambda s, n: k_ref[g, :n, :],
                lambda s, n: vt_ref[g * B_HEAD_DIM:(g + 1) * B_HEAD_DIM, :n], tab_ref.at[g], True, finish, BF16)

    _attend([job(g) for g in range(q_ref.shape[0])])


def _pair_attn_kernel(q_ref, k_ref, vt_ref, tab_ref, o_ref):
    def job(g):
        def finish(qi, outs):
            (a0, l0), (a1, l1) = outs
            o_ref[g, _key_tile(qi), :] = jnp.concatenate([a0 / l0, a1 / l1], axis=0).T.astype(BF16)

        return (lambda qi: [q_ref[2 * g, _key_tile(qi), :], q_ref[2 * g + 1, _key_tile(qi), :]],
                lambda s, n: k_ref[2 * g + s, :n, :],
                lambda s, n: vt_ref[(2 * g + s) * C_V_DIM:(2 * g + s + 1) * C_V_DIM, :n],
                tab_ref, False, finish, F32)

    _attend([job(g) for g in range(o_ref.shape[0])])


def _flash_call(kernel, name, q_arr, q_head0, k_arr, k_head0, vt_arr, tab, extra, extra_specs,
                batch, steps, heads_per_step, outs_per_step, tab_per_head):
    m = q_arr.shape[1]
    vrows = vt_arr.shape[1] // steps
    qk_spec = lambda h0: pl.BlockSpec((heads_per_step, SEQ, LANES),
                                      lambda h, b: (h0 // heads_per_step + h, b, 0))
    if tab_per_head:
        tab_spec = pl.BlockSpec((heads_per_step,) + tab.shape[1:], lambda h, b: (h, 0, 0))
    else:
        tab_spec = pl.BlockSpec((None,) + tab.shape[1:], lambda h, b: (0, 0, 0))
    in_specs = [
        qk_spec(q_head0),
        qk_spec(k_head0),
        pl.BlockSpec((None, vrows, SEQ), lambda h, b: (b, h, 0)),
        tab_spec,
    ] + extra_specs
    return pl.pallas_call(
        kernel,
        grid=(steps, batch),
        in_specs=in_specs,
        out_specs=pl.BlockSpec((outs_per_step, SEQ, LANES), lambda h, b: (h, b, 0)),
        out_shape=jax.ShapeDtypeStruct((steps * outs_per_step, m, LANES), BF16),
        compiler_params=pltpu.CompilerParams(
            dimension_semantics=("arbitrary", "arbitrary"), vmem_limit_bytes=VMEM_LIMIT),
        name=name,
    )(q_arr, k_arr, vt_arr, tab, *extra)


def _mix_kernel(x_ref, g_ref, oa_ref, ob_ref, oc_ref, wg_ref, wa_ref, wb_ref, wc_ref, wo_ref, o_ref):
    x = x_ref[...]
    h = _rms(x, g_ref[...]).astype(BF16)
    mix = None
    for n, (o_br, w_br) in enumerate(((oa_ref, wa_ref), (ob_ref, wb_ref), (oc_ref, wc_ref))):
        gate = jax.nn.sigmoid(_nt_dot(h, wg_ref[n * D_MODEL:(n + 1) * D_MODEL, :]))
        o = jnp.concatenate([o_br[hd] for hd in range(o_br.shape[0])], axis=1)
        term = gate * jnp.dot(o, w_br[...], preferred_element_type=F32)
        mix = term if mix is None else mix + term
    o_ref[...] = x + jnp.dot(mix.astype(BF16), wo_ref[...], preferred_element_type=F32)


def _mix(x2d, g, oa, ob, oc, wg, wa, wb, wc, wo, layer, tm):
    m = x2d.shape[0]
    row = lambda w: pl.BlockSpec((tm, w), lambda i: (i, 0))
    heads = lambda w: pl.BlockSpec((w // LANES, tm, LANES), lambda i: (0, i, 0))
    return pl.pallas_call(
        _mix_kernel,
        grid=(m // tm,),
        in_specs=[row(D_MODEL), _resident((1, D_MODEL)), heads(A_W), heads(B_W), heads(C_OUT_W),
                  _resident((3 * D_MODEL, D_MODEL), layer), _resident((A_W, D_MODEL), layer),
                  _resident((B_W, D_MODEL), layer), _resident((C_OUT_W, D_MODEL), layer),
                  _resident((D_MODEL, D_MODEL), layer)],
        out_specs=row(D_MODEL),
        out_shape=jax.ShapeDtypeStruct((m, D_MODEL), F32),
        compiler_params=pltpu.CompilerParams(
            dimension_semantics=("arbitrary",), vmem_limit_bytes=VMEM_LIMIT),
        name="mix",
    )(x2d, g, oa, ob, oc, wg, wa, wb, wc, wo)


def _ffn_kernel(final, x_ref, g_ref, wg_ref, wu_ref, wd_ref, fg_ref, o_ref):
    half = x_ref.shape[0] // 2
    for part in range(2):
        rows = slice(part * half, (part + 1) * half)
        x = x_ref[rows, :]
        h = _rms(x, g_ref[...]).astype(BF16)
        gate = jnp.dot(h, wg_ref[...], preferred_element_type=F32)
        up = jnp.dot(h, wu_ref[...], preferred_element_type=F32)
        act = (gate * jax.nn.sigmoid(gate) * up).astype(BF16)
        y = x + jnp.dot(act, wd_ref[...], preferred_element_type=F32)
        o_ref[rows, :] = _rms(y, fg_ref[...]) if final else y


def _ffn(x2d, g, wg, wu, wd, fg, layer, final, tm):
    m = x2d.shape[0]
    return pl.pallas_call(
        functools.partial(_ffn_kernel, final),
        grid=(m // tm,),
        in_specs=[
            pl.BlockSpec((tm, D_MODEL), lambda i: (i, 0)),
            _resident((1, D_MODEL)),
            _resident((D_MODEL, FFN_HIDDEN), layer),
            _resident((D_MODEL, FFN_HIDDEN), layer),
            _resident((FFN_HIDDEN, D_MODEL), layer),
            _resident((1, D_MODEL)),
        ],
        out_specs=pl.BlockSpec((tm, D_MODEL), lambda i: (i, 0)),
        out_shape=jax.ShapeDtypeStruct((m, D_MODEL), F32),
        compiler_params=pltpu.CompilerParams(
            dimension_semantics=("arbitrary",), vmem_limit_bytes=VMEM_LIMIT),
        name="ffn",
    )(x2d, g, wg, wu, wd, fg)


def _alibi_slopes():
    n = A_HEADS + B_HEADS
    s = 2.0 ** (-8.0 * np.arange(1, n + 1, dtype=np.float64) / n)
    return s[0::2], s[1::2]


def _bias_tables(slopes, multiplicity, rows):
    r = np.arange(rows, dtype=np.int64)[:, None]
    i = np.arange(T_ATT, dtype=np.int64)[None, :]
    d = (rows - T_ATT) - r + i
    mult = multiplicity(d)
    valid = (d >= 0) & (mult > 0)
    tab = np.log2(np.maximum(mult, 1))[None] - (np.asarray(slopes) * LOG2E)[:, None, None] * d[None]
    return jnp.asarray(np.where(valid[None], tab, NEG_INF).astype(np.float32))


def _kpos_tables(slopes):
    v = (np.asarray(slopes) * LOG2E).astype(np.float32)[:, None] * np.arange(SEQ, dtype=np.float32)[None, :]
    tab = np.zeros(v.shape + (LANES,), BF16)
    for term in range(KPOS_TERMS):
        piece = v.astype(BF16)
        tab[..., term] = piece
        v = v - piece.astype(np.float32)
    return jnp.asarray(tab)


def _dilated_multiplicity(d):
    mult = np.zeros_like(d)
    for window, dil in B_PATTERNS:
        mult = mult + ((d % dil == 0) & (d // dil <= window // dil))
    return mult


def _rope_tables():
    half = C_ROPE_DIM // 2
    inv_freq = ROPE_THETA ** (-np.arange(half, dtype=np.float64) / half)
    ang = np.arange(SEQ, dtype=np.float64)[:, None] * inv_freq[None, :]
    cos = np.concatenate([np.cos(ang)] * 2, axis=-1)
    sin = np.concatenate([np.sin(ang)] * 2, axis=-1)
    scale = (C_NOPE_DIM + C_ROPE_DIM) ** -0.5 * LOG2E
    qtab = np.concatenate([np.full((SEQ, C_NOPE_DIM), scale), cos * scale, sin * scale], axis=-1)
    ktab = np.concatenate([np.zeros((SEQ, C_NOPE_DIM)), cos, sin], axis=-1)
    return jnp.asarray(qtab.astype(np.float32)), jnp.asarray(ktab.astype(np.float32))


def _rotate_half(w, axis):
    first, second = jnp.split(w, 2, axis=axis)
    return jnp.concatenate([-second, first], axis=axis)


def _stacked_weights(w_in, w_uq, w_ukv):
    depth = w_in.shape[0]
    sa, sb = A_HEAD_DIM ** -0.5 * LOG2E, B_HEAD_DIM ** -0.5 * LOG2E
    b0 = 3 * A_W
    w_t = jnp.swapaxes(w_in, -1, -2)
    w_qk = jnp.concatenate([w_t[:, :2 * A_W], w_t[:, b0:b0 + 2 * B_W]], axis=1).astype(BF16)
    w_gate = w_t[:, GATE_OFF:].astype(BF16)
    colscale = jnp.concatenate(
        [jnp.full((A_W,), sa, F32), jnp.ones((A_W,), F32), jnp.full((B_W,), sb, F32),
         jnp.ones((B_W,), F32)])[None, :]
    w_v = jnp.concatenate([w_t[:, 2 * A_W:3 * A_W], w_t[:, b0 + 2 * B_W:b0 + 3 * B_W]], axis=1).astype(BF16)
    w_pe = w_t[:, C_PE_OFF:GATE_OFF]
    w_c = jnp.concatenate(
        [w_t[:, C_Q_OFF:C_PE_OFF], jnp.zeros((depth, C_NOPE_DIM, D_MODEL), F32), w_pe, _rotate_half(w_pe, 1)],
        axis=1).astype(BF16)
    uq = w_uq.reshape(depth, C_Q_LORA, C_HEADS, C_NOPE_DIM + C_ROPE_DIM)
    uq_pe = uq[..., C_NOPE_DIM:]
    w_uq_ext = jnp.concatenate([uq, _rotate_half(uq_pe, -1)], axis=-1).reshape(
        depth, C_Q_LORA, C_HEADS * LANES).astype(BF16)
    ukv = w_ukv.reshape(depth, C_KV_LORA, C_HEADS, C_NOPE_DIM + C_V_DIM)
    w_uk = jnp.concatenate(
        [ukv[..., :C_NOPE_DIM], jnp.zeros((depth, C_KV_LORA, C_HEADS, LANES - C_NOPE_DIM), F32)],
        axis=-1).reshape(depth, C_KV_LORA, C_HEADS * LANES).astype(BF16)
    w_uv = ukv[..., C_NOPE_DIM:].reshape(depth, C_KV_LORA, C_OUT_W).astype(BF16)
    return w_qk, colscale, w_gate, w_v, w_c, w_uq_ext, w_uk, w_uv


def kernel(x, attn_norm, w_in, diff_lambda, diff_norm, mla_q_norm, mla_w_uq, mla_kv_norm, mla_w_ukv,
           w_branch_a, w_branch_b, w_branch_c, w_out, ffn_norm, w_ffn_gate, w_ffn_up, w_ffn_down,
           final_norm):
    batch, seq, d_model = x.shape
    assert (seq, d_model) == (SEQ, D_MODEL)
    m = batch * seq
    tm_proj, tm_ffn = 1024, 512

    slopes_a, slopes_b = _alibi_slopes()
    kpos_a = _kpos_tables(slopes_a)
    tab_b = _bias_tables(slopes_b, _dilated_multiplicity, SEQ)
    tab_c = _bias_tables(np.zeros((1,)), np.ones_like, T_ATT)
    qtab, ktab = _rope_tables()

    wa, wb, wc, wo = (w.astype(BF16) for w in (w_branch_a, w_branch_b, w_branch_c, w_out))
    wfg, wfu, wfd = (w.astype(BF16) for w in (w_ffn_gate, w_ffn_up, w_ffn_down))

    w_qk, colscale, w_gate, w_v, w_c, w_uq_ext, w_uk, w_uv = _stacked_weights(w_in, mla_w_uq, mla_w_ukv)

    x2d = x.reshape(m, D_MODEL)
    for l in range(DEPTH):
        g_attn = attn_norm[l][None, :]
        qk, vta, vtb, qc, kc, vtc = _proj(
            x2d, g_attn, w_qk, colscale, w_v, w_c, mla_q_norm[l][None, :], mla_kv_norm[l][None, :],
            w_uq_ext, w_uk, w_uv, qtab, ktab, l, batch, tm_proj)

        lam_init = 0.8 - 0.6 * math.exp(-0.3 * l)
        oa = _flash_call(
            functools.partial(_diff_attn_kernel, lam_init), "diff_attn", qk, 0, qk, A_HEADS, vta, tab_c,
            [kpos_a, diff_lambda[l], diff_norm[l][:, None]],
            [pl.BlockSpec((HEADS_PER_STEP, SEQ, LANES), lambda h, b: (h, 0, 0)),
             pl.BlockSpec((4, A_HEAD_DIM), lambda h, b: (0, 0)),
             pl.BlockSpec((2 * A_HEAD_DIM, 1), lambda h, b: (0, 0))],
            batch, A_HEADS // HEADS_PER_STEP, HEADS_PER_STEP, HEADS_PER_STEP, False)
        ob = _flash_call(_single_attn_kernel, "dilated_attn", qk, 2 * A_HEADS, qk, 2 * A_HEADS + B_HEADS, vtb, tab_b,
                         [], [], batch, B_HEADS // HEADS_PER_STEP, HEADS_PER_STEP, HEADS_PER_STEP, True)
        oc = _flash_call(_pair_attn_kernel, "mla_attn", qc, 0, kc, 0, vtc, tab_c,
                         [], [], batch, C_HEADS // (2 * HEADS_PER_STEP), 2 * HEADS_PER_STEP, HEADS_PER_STEP, False)

        x2d = _mix(x2d, g_attn, oa, ob, oc, w_gate, wa, wb, wc, wo, l, tm_proj)
        x2d = _ffn(x2d, ffn_norm[l][None, :], wfg, wfu, wfd, final_norm[None, :], l, l == DEPTH - 1, tm_ffn)
    return x2d.reshape(batch, seq, d_model)
```

```python
import functools
import math

import jax
import jax.numpy as jnp
import numpy as np
from jax import lax
from jax.experimental import pallas as pl
from jax.experimental.pallas import tpu as pltpu

D_MODEL = 1024
SEQ = 2048
DEPTH = 2
A_HEADS = 4
A_HEAD_DIM = 64
B_HEADS = 4
B_HEAD_DIM = 128
B_PATTERNS = ((128, 1), (512, 4), (2048, 16))
C_HEADS = 8
C_NOPE_DIM = 64
C_ROPE_DIM = 32
C_V_DIM = 64
C_Q_LORA = 384
C_KV_LORA = 256
ROPE_THETA = 10000.0
FFN_HIDDEN = 2816
A_W = A_HEADS * 2 * A_HEAD_DIM
B_W = B_HEADS * B_HEAD_DIM
C_OUT_W = C_HEADS * C_V_DIM
C_Q_OFF = 2 * A_W + A_W + 3 * B_W
C_KV_OFF = C_Q_OFF + C_Q_LORA
C_PE_OFF = C_KV_OFF + C_KV_LORA
GATE_OFF = C_PE_OFF + C_ROPE_DIM
NORM_EPS = 1e-6
NEG_INF = -1e30
LOG2E = 1.4426950408889634

LANES = 128
T_ATT = 256
VMEM_LIMIT = 48 * 1024 * 1024

BF16 = jnp.bfloat16
F32 = jnp.float32


def _rms(xf, g):
    return xf * lax.rsqrt(jnp.mean(xf * xf, axis=-1, keepdims=True) + NORM_EPS) * g


def _nt_dot(a, b):
    return lax.dot_general(a, b, (((1,), (1,)), ((), ())), preferred_element_type=F32)


def _tn_dot(a, b):
    return lax.dot_general(a, b, (((0,), (1,)), ((), ())), preferred_element_type=F32)


QK_W = 2 * A_W + 2 * B_W
C_PE_GROUP = C_Q_LORA + C_KV_LORA
C_CW = C_PE_GROUP + LANES


def _resident(shape, layer=None):
    if layer is None:
        return pl.BlockSpec(shape, lambda *_: (0,) * len(shape), pipeline_mode=pl.Buffered(1))
    return pl.BlockSpec((None,) + shape, lambda *_: (layer,) + (0,) * len(shape),
                        pipeline_mode=pl.Buffered(1))


def _proj_kernel(x_ref, g_ref, wqk_ref, cs_ref, wv_ref, wc_ref, qn_ref, kvn_ref, wuq_ref, wuk_ref,
                 wuv_ref, qtab_ref, ktab_ref, qk_ref, vta_ref, vtb_ref, qc_ref, kc_ref, vtc_ref):
    h = _rms(x_ref[...], g_ref[...]).astype(BF16)

    qk = (_nt_dot(h, wqk_ref[...]) * cs_ref[...]).astype(BF16)
    for hd in range(QK_W // LANES):
        qk_ref[hd] = qk[:, hd * LANES:(hd + 1) * LANES]

    vt = _nt_dot(wv_ref[...], h).astype(BF16)
    vta_ref[...] = vt[:A_W]
    vtb_ref[...] = vt[A_W:]

    c = _nt_dot(h, wc_ref[...])
    cqn = _rms(c[:, :C_Q_LORA], qn_ref[...]).astype(BF16)
    ckvn = _rms(c[:, C_Q_LORA:C_PE_GROUP], kvn_ref[...]).astype(BF16)

    q = jnp.dot(cqn, wuq_ref[...], preferred_element_type=F32)
    qtab = qtab_ref[...]
    for hd in range(C_HEADS):
        sl = slice(hd * LANES, (hd + 1) * LANES)
        qc_ref[hd] = (q[:, sl] * qtab).astype(BF16)

    kt = c[:, C_PE_GROUP:] * ktab_ref[...]
    lane = lax.broadcasted_iota(jnp.int32, kt.shape, 1)
    swapped = jnp.where(lane < C_NOPE_DIM + C_ROPE_DIM, pltpu.roll(kt, LANES - C_ROPE_DIM, 1),
                        pltpu.roll(kt, C_ROPE_DIM, 1))
    kp2 = jnp.where(lane >= C_NOPE_DIM, kt + swapped, 0.0)
    kk = jnp.dot(ckvn, wuk_ref[...], preferred_element_type=F32)
    for hd in range(C_HEADS):
        sl = slice(hd * LANES, (hd + 1) * LANES)
        kc_ref[hd] = (kk[:, sl] + kp2).astype(BF16)

    vtc_ref[...] = _tn_dot(wuv_ref[...], ckvn).astype(BF16)


def _proj(x2d, g, wqk, colscale, wv, wc, qn, kvn, wuq, wuk, wuv, qtab, ktab, layer, batch, tm):
    m = x2d.shape[0]
    per_seq = SEQ // tm
    row = lambda w: pl.BlockSpec((tm, w), lambda i: (i, 0))
    pos = pl.BlockSpec((tm, LANES), lambda i: (i % per_seq, 0))
    vt_spec = lambda w: pl.BlockSpec((None, w, tm), lambda i: (i // per_seq, 0, i % per_seq))
    heads_spec = lambda nh: pl.BlockSpec((nh, tm, LANES), lambda i: (0, i, 0))
    return pl.pallas_call(
        _proj_kernel,
        grid=(m // tm,),
        in_specs=[
            row(D_MODEL),
            _resident((1, D_MODEL)),
            _resident((QK_W, D_MODEL), layer),
            _resident((1, QK_W)),
            _resident((A_W + B_W, D_MODEL), layer),
            _resident((C_CW, D_MODEL), layer),
            _resident((1, C_Q_LORA)),
            _resident((1, C_KV_LORA)),
            _resident((C_Q_LORA, C_HEADS * LANES), layer),
            _resident((C_KV_LORA, C_HEADS * LANES), layer),
            _resident((C_KV_LORA, C_OUT_W), layer),
            pos, pos,
        ],
        out_specs=[heads_spec(QK_W // LANES), vt_spec(A_W), vt_spec(B_W), heads_spec(C_HEADS),
                   heads_spec(C_HEADS), vt_spec(C_OUT_W)],
        out_shape=[
            jax.ShapeDtypeStruct((QK_W // LANES, m, LANES), BF16),
            jax.ShapeDtypeStruct((batch, A_W, SEQ), BF16),
            jax.ShapeDtypeStruct((batch, B_W, SEQ), BF16),
            jax.ShapeDtypeStruct((C_HEADS, m, LANES), BF16),
            jax.ShapeDtypeStruct((C_HEADS, m, LANES), BF16),
            jax.ShapeDtypeStruct((batch, C_OUT_W, SEQ), BF16),
        ],
        compiler_params=pltpu.CompilerParams(
            dimension_semantics=("arbitrary",), vmem_limit_bytes=VMEM_LIMIT),
        name="proj",
    )(x2d, g, wqk, colscale, wv, wc, qn, kvn, wuq, wuk, wuv, qtab, ktab)


NQ_ATT = SEQ // T_ATT
KPOS_TERMS = 3
HEADS_PER_STEP = 4
SUM_ROWS = 16


def _key_tile(jt):
    return slice(jt * T_ATT, (jt + 1) * T_ATT)


def _causal_exp2(st, m, exp_dtype):
    n, half = st.shape[0], T_ATT // 2
    live = jnp.exp2((st[:n - half] - m).astype(exp_dtype)).astype(BF16)
    corner = jnp.exp2((st[n - half:, half:] - m[:, half:]).astype(exp_dtype)).astype(BF16)
    dead = jnp.zeros((half, half), BF16)
    return jnp.concatenate([live, jnp.concatenate([dead, corner], axis=1)], axis=0)


def _attend(jobs):
    tiles = [(job, t) for job in jobs for t in range(NQ_ATT)]
    scored = None
    probs = None
    for step in range(len(tiles) + 2):
        new_scored = None
        if step < len(tiles):
            job, t = tiles[step]
            qs_of, k_rows, _, tab_ref, bias_off_diag, _, _ = job
            n = (t + 1) * T_ATT
            outs = []
            for s, q in enumerate(qs_of(t)):
                st = _nt_dot(k_rows(s, n), q)
                if bias_off_diag:
                    st = st + tab_ref[tab_ref.shape[0] - n:, :]
                elif t == 0:
                    st = st + tab_ref[...]
                else:
                    st = jnp.concatenate([st[:n - T_ATT], st[n - T_ATT:] + tab_ref[...]], axis=0)
                outs.append((st, jnp.max(st, axis=0, keepdims=True)))
            new_scored = (job, t, outs)
        new_probs = None
        if scored is not None:
            new_probs = scored[:2] + ([_causal_exp2(st, m, scored[0][6]) for st, m in scored[2]],)
        if probs is not None:
            job, t, ps = probs
            _, _, vt_cols, _, _, finish, _ = job
            n = (t + 1) * T_ATT
            ones_rows = (lax.broadcasted_iota(jnp.int32, (SUM_ROWS, n), 0) == 0).astype(BF16)
            outs = []
            for s, p in enumerate(ps):
                vt = jnp.concatenate([vt_cols(s, n), ones_rows], axis=0)
                a = jnp.dot(vt, p, preferred_element_type=F32)
                outs.append((a[:-SUM_ROWS], a[-SUM_ROWS:-SUM_ROWS + 1]))
            finish(t, outs)
        scored, probs = new_scored, new_probs


def _diff_attn_kernel(lam_init, q_ref, k_ref, vt_ref, tab_ref, kpos_ref, dl_ref, gain_ref, o_ref):
    dl = dl_ref[...]
    lam = (jnp.exp(jnp.sum(dl[0:1] * dl[1:2], axis=-1, keepdims=True))
           - jnp.exp(jnp.sum(dl[2:3] * dl[3:4], axis=-1, keepdims=True)) + lam_init)
    gain = gain_ref[...] * (1.0 - lam_init)
    lane = lax.broadcasted_iota(jnp.int32, (T_ATT, LANES), 1)
    q_ones = (lane < KPOS_TERMS).astype(BF16)
    dv = 2 * A_HEAD_DIM

    def job(g):
        def qs_of(qi):
            q = q_ref[g, _key_tile(qi), :]
            zero = jnp.zeros_like(q)
            return [jnp.concatenate([jnp.where(lane < A_HEAD_DIM, q, zero), q_ones], axis=1),
                    jnp.concatenate([jnp.where(lane >= A_HEAD_DIM, q, zero), q_ones], axis=1)]

        def finish(qi, outs):
            (a0, l0), (a1, l1) = outs
            o = a0 / l0 - lam * (a1 / l1)
            y = o * lax.rsqrt(jnp.mean(o * o, axis=0, keepdims=True) + NORM_EPS) * gain
            o_ref[g, _key_tile(qi), :] = y.T.astype(BF16)

        return (qs_of, lambda s, n: jnp.concatenate([k_ref[g, :n, :], kpos_ref[g, :n, :]], axis=1),
                lambda s, n: vt_ref[g * dv:(g + 1) * dv, :n], tab_ref, False, finish, BF16)

    _attend([job(g) for g in range(q_ref.shape[0])])


def _single_attn_kernel(q_ref, k_ref, vt_ref, tab_ref, o_ref):
    def job(g):
        def finish(qi, outs):
            (a, l), = outs
            o_ref[g, _key_tile(qi), :] = (a / l).T.astype(BF16)

        return (lambda qi: [q_ref[g, _key_tile(qi), :]], lambda s, n: k_ref[g, :n, :],
                lambda s, n: vt_ref[g * B_HEAD_DIM:(g + 1) * B_HEAD_DIM, :n], tab_ref.at[g], True, finish, BF16)

    _attend([job(g) for g in range(q_ref.shape[0])])


def _pair_attn_kernel(q_ref, k_ref, vt_ref, tab_ref, o_ref):
    def job(g):
        def finish(qi, outs):
            (a0, l0), (a1, l1) = outs
            o_ref[g, _key_tile(qi), :] = jnp.concatenate([a0 / l0, a1 / l1], axis=0).T.astype(BF16)

        return (lambda qi: [q_ref[2 * g, _key_tile(qi), :], q_ref[2 * g + 1, _key_tile(qi), :]],
                lambda s, n: k_ref[2 * g + s, :n, :],
                lambda s, n: vt_ref[(2 * g + s) * C_V_DIM:(2 * g + s + 1) * C_V_DIM, :n],
                tab_ref, False, finish, BF16)

    _attend([job(g) for g in range(o_ref.shape[0])])


def _flash_call(kernel, name, q_arr, q_head0, k_arr, k_head0, vt_arr, tab, extra, extra_specs,
                batch, steps, heads_per_step, outs_per_step, tab_per_head):
    m = q_arr.shape[1]
    vrows = vt_arr.shape[1] // steps
    qk_spec = lambda h0: pl.BlockSpec((heads_per_step, SEQ, LANES),
                                      lambda h, b: (h0 // heads_per_step + h, b, 0))
    if tab_per_head:
        tab_spec = pl.BlockSpec((heads_per_step,) + tab.shape[1:], lambda h, b: (h, 0, 0))
    else:
        tab_spec = pl.BlockSpec((None,) + tab.shape[1:], lambda h, b: (0, 0, 0))
    in_specs = [
        qk_spec(q_head0),
        qk_spec(k_head0),
        pl.BlockSpec((None, vrows, SEQ), lambda h, b: (b, h, 0)),
        tab_spec,
    ] + extra_specs
    return pl.pallas_call(
        kernel,
        grid=(steps, batch),
        in_specs=in_specs,
        out_specs=pl.BlockSpec((outs_per_step, SEQ, LANES), lambda h, b: (h, b, 0)),
        out_shape=jax.ShapeDtypeStruct((steps * outs_per_step, m, LANES), BF16),
        compiler_params=pltpu.CompilerParams(
            dimension_semantics=("arbitrary", "arbitrary"), vmem_limit_bytes=VMEM_LIMIT),
        name=name,
    )(q_arr, k_arr, vt_arr, tab, *extra)


def _mix_kernel(x_ref, g_ref, oa_ref, ob_ref, oc_ref, wg_ref, wa_ref, wb_ref, wc_ref, wo_ref, o_ref):
    x = x_ref[...]
    h = _rms(x, g_ref[...]).astype(BF16)
    mix = None
    for n, (o_br, w_br) in enumerate(((oa_ref, wa_ref), (ob_ref, wb_ref), (oc_ref, wc_ref))):
        gate = jax.nn.sigmoid(_nt_dot(h, wg_ref[n * D_MODEL:(n + 1) * D_MODEL, :]))
        o = jnp.concatenate([o_br[hd] for hd in range(o_br.shape[0])], axis=1)
        term = gate * jnp.dot(o, w_br[...], preferred_element_type=F32)
        mix = term if mix is None else mix + term
    o_ref[...] = x + jnp.dot(mix.astype(BF16), wo_ref[...], preferred_element_type=F32)


def _mix(x2d, g, oa, ob, oc, wg, wa, wb, wc, wo, layer, tm):
    m = x2d.shape[0]
    row = lambda w: pl.BlockSpec((tm, w), lambda i: (i, 0))
    heads = lambda w: pl.BlockSpec((w // LANES, tm, LANES), lambda i: (0, i, 0))
    return pl.pallas_call(
        _mix_kernel,
        grid=(m // tm,),
        in_specs=[row(D_MODEL), _resident((1, D_MODEL)), heads(A_W), heads(B_W), heads(C_OUT_W),
                  _resident((3 * D_MODEL, D_MODEL), layer), _resident((A_W, D_MODEL), layer),
                  _resident((B_W, D_MODEL), layer), _resident((C_OUT_W, D_MODEL), layer),
                  _resident((D_MODEL, D_MODEL), layer)],
        out_specs=row(D_MODEL),
        out_shape=jax.ShapeDtypeStruct((m, D_MODEL), F32),
        compiler_params=pltpu.CompilerParams(
            dimension_semantics=("arbitrary",), vmem_limit_bytes=VMEM_LIMIT),
        name="mix",
    )(x2d, g, oa, ob, oc, wg, wa, wb, wc, wo)


def _ffn_kernel(final, x_ref, g_ref, wg_ref, wu_ref, wd_ref, fg_ref, o_ref):
    half = x_ref.shape[0] // 2
    for part in range(2):
        rows = slice(part * half, (part + 1) * half)
        x = x_ref[rows, :]
        h = _rms(x, g_ref[...]).astype(BF16)
        gate = jnp.dot(h, wg_ref[...], preferred_element_type=F32)
        up = jnp.dot(h, wu_ref[...], preferred_element_type=F32)
        act = (gate * jax.nn.sigmoid(gate) * up).astype(BF16)
        y = x + jnp.dot(act, wd_ref[...], preferred_element_type=F32)
        o_ref[rows, :] = _rms(y, fg_ref[...]) if final else y


def _ffn(x2d, g, wg, wu, wd, fg, layer, final, tm):
    m = x2d.shape[0]
    return pl.pallas_call(
        functools.partial(_ffn_kernel, final),
        grid=(m // tm,),
        in_specs=[
            pl.BlockSpec((tm, D_MODEL), lambda i: (i, 0)),
            _resident((1, D_MODEL)),
            _resident((D_MODEL, FFN_HIDDEN), layer),
            _resident((D_MODEL, FFN_HIDDEN), layer),
            _resident((FFN_HIDDEN, D_MODEL), layer),
            _resident((1, D_MODEL)),
        ],
        out_specs=pl.BlockSpec((tm, D_MODEL), lambda i: (i, 0)),
        out_shape=jax.ShapeDtypeStruct((m, D_MODEL), F32),
        compiler_params=pltpu.CompilerParams(
            dimension_semantics=("arbitrary",), vmem_limit_bytes=VMEM_LIMIT),
        name="ffn",
    )(x2d, g, wg, wu, wd, fg)


def _alibi_slopes():
    n = A_HEADS + B_HEADS
    s = 2.0 ** (-8.0 * np.arange(1, n + 1, dtype=np.float64) / n)
    return s[0::2], s[1::2]


def _bias_tables(slopes, multiplicity, rows):
    r = np.arange(rows, dtype=np.int64)[:, None]
    i = np.arange(T_ATT, dtype=np.int64)[None, :]
    d = (rows - T_ATT) - r + i
    mult = multiplicity(d)
    valid = (d >= 0) & (mult > 0)
    tab = np.log2(np.maximum(mult, 1))[None] - (np.asarray(slopes) * LOG2E)[:, None, None] * d[None]
    return jnp.asarray(np.where(valid[None], tab, NEG_INF).astype(np.float32))


def _kpos_tables(slopes):
    v = (np.asarray(slopes) * LOG2E).astype(np.float32)[:, None] * np.arange(SEQ, dtype=np.float32)[None, :]
    tab = np.zeros(v.shape + (LANES,), BF16)
    for term in range(KPOS_TERMS):
        piece = v.astype(BF16)
        tab[..., term] = piece
        v = v - piece.astype(np.float32)
    return jnp.asarray(tab)


def _dilated_multiplicity(d):
    mult = np.zeros_like(d)
    for window, dil in B_PATTERNS:
        mult = mult + ((d % dil == 0) & (d // dil <= window // dil))
    return mult


def _rope_tables():
    half = C_ROPE_DIM // 2
    inv_freq = ROPE_THETA ** (-np.arange(half, dtype=np.float64) / half)
    ang = np.arange(SEQ, dtype=np.float64)[:, None] * inv_freq[None, :]
    cos = np.concatenate([np.cos(ang)] * 2, axis=-1)
    sin = np.concatenate([np.sin(ang)] * 2, axis=-1)
    scale = (C_NOPE_DIM + C_ROPE_DIM) ** -0.5 * LOG2E
    qtab = np.concatenate([np.full((SEQ, C_NOPE_DIM), scale), cos * scale, sin * scale], axis=-1)
    ktab = np.concatenate([np.zeros((SEQ, C_NOPE_DIM)), cos, sin], axis=-1)
    return jnp.asarray(qtab.astype(np.float32)), jnp.asarray(ktab.astype(np.float32))


def _rotate_half(w, axis):
    first, second = jnp.split(w, 2, axis=axis)
    return jnp.concatenate([-second, first], axis=axis)


def _stacked_weights(w_in, w_uq, w_ukv):
    depth = w_in.shape[0]
    sa, sb = A_HEAD_DIM ** -0.5 * LOG2E, B_HEAD_DIM ** -0.5 * LOG2E
    b0 = 3 * A_W
    w_t = jnp.swapaxes(w_in, -1, -2)
    w_qk = jnp.concatenate([w_t[:, :2 * A_W], w_t[:, b0:b0 + 2 * B_W]], axis=1).astype(BF16)
    w_gate = w_t[:, GATE_OFF:].astype(BF16)
    colscale = jnp.concatenate(
        [jnp.full((A_W,), sa, F32), jnp.ones((A_W,), F32), jnp.full((B_W,), sb, F32),
         jnp.ones((B_W,), F32)])[None, :]
    w_v = jnp.concatenate([w_t[:, 2 * A_W:3 * A_W], w_t[:, b0 + 2 * B_W:b0 + 3 * B_W]], axis=1).astype(BF16)
    w_pe = w_t[:, C_PE_OFF:GATE_OFF]
    w_c = jnp.concatenate(
        [w_t[:, C_Q_OFF:C_PE_OFF], jnp.zeros((depth, C_NOPE_DIM, D_MODEL), F32), w_pe, _rotate_half(w_pe, 1)],
        axis=1).astype(BF16)
    uq = w_uq.reshape(depth, C_Q_LORA, C_HEADS, C_NOPE_DIM + C_ROPE_DIM)
    uq_pe = uq[..., C_NOPE_DIM:]
    w_uq_ext = jnp.concatenate([uq, _rotate_half(uq_pe, -1)], axis=-1).reshape(
        depth, C_Q_LORA, C_HEADS * LANES).astype(BF16)
    ukv = w_ukv.reshape(depth, C_KV_LORA, C_HEADS, C_NOPE_DIM + C_V_DIM)
    w_uk = jnp.concatenate(
        [ukv[..., :C_NOPE_DIM], jnp.zeros((depth, C_KV_LORA, C_HEADS, LANES - C_NOPE_DIM), F32)],
        axis=-1).reshape(depth, C_KV_LORA, C_HEADS * LANES).astype(BF16)
    w_uv = ukv[..., C_NOPE_DIM:].reshape(depth, C_KV_LORA, C_OUT_W).astype(BF16)
    return w_qk, colscale, w_gate, w_v, w_c, w_uq_ext, w_uk, w_uv


def kernel(x, attn_norm, w_in, diff_lambda, diff_norm, mla_q_norm, mla_w_uq, mla_kv_norm, mla_w_ukv,
           w_branch_a, w_branch_b, w_branch_c, w_out, ffn_norm, w_ffn_gate, w_ffn_up, w_ffn_down,
           final_norm):
    batch, seq, d_model = x.shape
    assert (seq, d_model) == (SEQ, D_MODEL)
    m = batch * seq
    tm_proj, tm_ffn = 1024, 512

    slopes_a, slopes_b = _alibi_slopes()
    kpos_a = _kpos_tables(slopes_a)
    tab_b = _bias_tables(slopes_b, _dilated_multiplicity, SEQ)
    tab_c = _bias_tables(np.zeros((1,)), np.ones_like, T_ATT)
    qtab, ktab = _rope_tables()

    wa, wb, wc, wo = (w.astype(BF16) for w in (w_branch_a, w_branch_b, w_branch_c, w_out))
    wfg, wfu, wfd = (w.astype(BF16) for w in (w_ffn_gate, w_ffn_up, w_ffn_down))

    w_qk, colscale, w_gate, w_v, w_c, w_uq_ext, w_uk, w_uv = _stacked_weights(w_in, mla_w_uq, mla_w_ukv)

    x2d = x.reshape(m, D_MODEL)
    for l in range(DEPTH):
        g_attn = attn_norm[l][None, :]
        qk, vta, vtb, qc, kc, vtc = _proj(
            x2d, g_attn, w_qk, colscale, w_v, w_c, mla_q_norm[l][None, :], mla_kv_norm[l][None, :],
            w_uq_ext, w_uk, w_uv, qtab, ktab, l, batch, tm_proj)

        lam_init = 0.8 - 0.6 * math.exp(-0.3 * l)
        oa = _flash_call(
            functools.partial(_diff_attn_kernel, lam_init), "diff_attn", qk, 0, qk, A_HEADS, vta, tab_c,
            [kpos_a, diff_lambda[l], diff_norm[l][:, None]],
            [pl.BlockSpec((HEADS_PER_STEP, SEQ, LANES), lambda h, b: (h, 0, 0)),
             pl.BlockSpec((4, A_HEAD_DIM), lambda h, b: (0, 0)),
             pl.BlockSpec((2 * A_HEAD_DIM, 1), lambda h, b: (0, 0))],
            batch, A_HEADS // HEADS_PER_STEP, HEADS_PER_STEP, HEADS_PER_STEP, False)
        ob = _flash_call(_single_attn_kernel, "dilated_attn", qk, 2 * A_HEADS, qk, 2 * A_HEADS + B_HEADS, vtb, tab_b,
                         [], [], batch, B_HEADS // HEADS_PER_STEP, HEADS_PER_STEP, HEADS_PER_STEP, True)
        oc = _flash_call(_pair_attn_kernel, "mla_attn", qc, 0, kc, 0, vtc, tab_c,
                         [], [], batch, C_HEADS // (2 * HEADS_PER_STEP), 2 * HEADS_PER_STEP, HEADS_PER_STEP, False)

        x2d = _mix(x2d, g_attn, oa, ob, oc, w_gate, wa, wb, wc, wo, l, tm_proj)
        x2d = _ffn(x2d, ffn_norm[l][None, :], wfg, wfu, wfd, final_norm[None, :], l, l == DEPTH - 1, tm_ffn)
    return x2d.reshape(batch, seq, d_model)
```
